```python
import jax, jax.numpy as jnp
from jax import lax
import numpy as np

D_MODEL = 1024
BATCH = 4
SEQ = 8192
DEPTH = 2
DEC_BATCH = 8
DEC_SEQ = 16
PAST_LEN = 4096

CHUNK = 64
HG_HEADS = 4
HG_DK = 128
HG_DV = 128
HG_WIDTH = HG_HEADS * HG_DV
RW_HEADS = 8
RW_HEAD = 64
RW_WIDTH = RW_HEADS * RW_HEAD
MIX_WIDTH = HG_WIDTH + RW_WIDTH
HG_PROJ = 2 * HG_HEADS * HG_DK + 2 * HG_WIDTH
RW_PROJ = 3 * RW_WIDTH
IN_WIDTH = HG_PROJ + RW_PROJ
RW_DECAY_LORA = 64
RW_ICLR_LORA = 64
RW_VRES_LORA = 32
RW_GATE_LORA = 128
N_GROUPS = 4
EXPERTS_PER_GROUP = 8
N_EXPERTS = N_GROUPS * EXPERTS_PER_GROUP
TOP_K = 2
D_EXPERT = 512
MOE_BLOCK = 128
NORM_EPS = 1e-6
RW_GN_EPS = 64e-5

kernel_name = 'hybrid_hgrn2_rwkv7_hmoe_stream_step'


def rmsnorm(x, w):
    xf = x.astype(jnp.float32)
    y = xf * lax.rsqrt(jnp.mean(xf * xf, axis=-1, keepdims=True) + NORM_EPS)
    return (y * w.astype(jnp.float32)).astype(x.dtype)


def shift_right(seq, first):
    return jnp.concatenate([first[:, None, :], seq[:, :-1, :]], axis=1)


def hgrn2_chunked(q, k, v, logf, S0):
    B, L, H, DK = q.shape
    DV = v.shape[-1]
    C = min(L, CHUNK)
    n = L // C

    def blocks(t):
        return t.reshape(B, n, C, H, t.shape[-1]).transpose(1, 0, 3, 2, 4)

    causal = jnp.tril(jnp.ones((C, C), dtype=bool))[:, :, None]

    def step(S, blk):
        qc, kc, vc, gc = blk
        b = jnp.cumsum(gc, axis=2)
        diff = b[:, :, :, None, :] - b[:, :, None, :, :]
        decay = jnp.exp(jnp.where(causal, diff, -jnp.inf))
        scores = jnp.einsum('bhtd,bhsd,bhtsd->bhts', qc, kc, decay)
        o = (jnp.einsum('bhts,bhsv->bhtv', scores, vc)
             + jnp.einsum('bhtd,bhdv->bhtv', qc * jnp.exp(b), S))
        b_last = b[:, :, -1:, :]
        S_new = (jnp.exp(b_last[:, :, 0, :])[..., None] * S
                 + jnp.einsum('bhsd,bhsv->bhdv', kc * jnp.exp(b_last - b), vc))
        return S_new, o

    S_fin, o = lax.scan(step, S0, (blocks(q), blocks(k), blocks(v), blocks(logf)))
    o = o.transpose(1, 0, 3, 2, 4).reshape(B, L, H, DV)
    return o, S_fin


def rwkv7_scan(r, logw, k, v, kk, a, S0):
    def step(S, inp):
        r_t, lw_t, k_t, v_t, kk_t, a_t = inp
        sa = jnp.einsum('bhij,bhj->bhi', S, -kk_t)
        S = (S * jnp.exp(lw_t)[:, :, None, :]
             + sa[..., None] * (kk_t * a_t)[:, :, None, :]
             + v_t[..., None] * k_t[:, :, None, :])
        y = jnp.einsum('bhij,bhj->bhi', S, r_t)
        return S, y

    xs = tuple(t.transpose(1, 0, 2, 3) for t in (r, logw, k, v, kk, a))
    S_fin, y = lax.scan(step, S0, xs)
    return y.transpose(1, 0, 2, 3), S_fin


def token_mixers(h, h_prev, S_hg0, S_rw0, v_first, l, lb, p):
    B, L, _ = h.shape
    f32 = jnp.float32
    w_in = p['w_in'][l]
    proj = h @ w_in

    qk_w = HG_HEADS * HG_DK
    hq, hf, hi, hgate = jnp.split(proj[..., :HG_PROJ], [qk_w, 2 * qk_w, 2 * qk_w + HG_WIDTH], axis=-1)
    q = jax.nn.silu(hq.astype(f32)).reshape(B, L, HG_HEADS, HG_DK)
    lb_l = lb[l]
    logf = jnp.logaddexp(jnp.log(lb_l), jnp.log1p(-lb_l) + jax.nn.log_sigmoid(hf.astype(f32)))
    logf = logf.reshape(B, L, HG_HEADS, HG_DK)
    k_hg = -jnp.expm1(logf)
    i_hg = hi.astype(f32).reshape(B, L, HG_HEADS, HG_DV)
    o_hg, S_hg = hgrn2_chunked(q, k_hg, i_hg, logf, S_hg0.astype(f32))
    o_hg = rmsnorm(o_hg, p['hg_norm_w'][l].reshape(HG_HEADS, HG_DV)).reshape(B, L, HG_WIDTH)
    o_hg = (o_hg * jax.nn.silu(hgate.astype(f32))).astype(h.dtype)

    P = proj[..., HG_PROJ:]
    P_prev = shift_right(P, h_prev @ w_in[:, HG_PROJ:])
    P = P + (P_prev - P) * p['rw_mu_proj'][l]
    r, k, v = jnp.split(P, 3, axis=-1)
    dh = shift_right(h, h_prev) - h
    mu = p['rw_mu_lora'][l]
    xw = h + dh * mu[0]
    xa = h + dh * mu[1]
    xg = h + dh * mu[2]
    w_pre = p['rw_w0'][l] + jnp.tanh(xw @ p['rw_w1'][l]) @ p['rw_w2'][l]
    logw = -jnp.exp(-jax.nn.softplus(-w_pre.astype(f32)) - 0.5)
    a = jax.nn.sigmoid((p['rw_a0'][l] + (xa @ p['rw_a1'][l]) @ p['rw_a2'][l]).astype(f32))
    g = jax.nn.sigmoid(xg @ p['rw_g1'][l]) @ p['rw_g2'][l]
    if l == 0:
        v_first = v
    else:
        xv = h + dh * p['rw_mu_vres'][l - 1]
        nu = jax.nn.sigmoid(p['rw_v0'][l - 1] + (xv @ p['rw_v1'][l - 1]) @ p['rw_v2'][l - 1])
        v = v + (v_first - v) * nu

    def heads(t):
        return t.astype(f32).reshape(B, L, RW_HEADS, RW_HEAD)

    r_h, k_h, v_h, a_h, lw_h = heads(r), heads(k), heads(v), heads(a), heads(logw)
    kk = k_h * p['rw_kk'][l].reshape(RW_HEADS, RW_HEAD)
    kk = kk / jnp.maximum(jnp.sqrt(jnp.sum(kk * kk, axis=-1, keepdims=True)), 1e-12)
    k_h = k_h * (1.0 + (a_h - 1.0) * p['rw_ka'][l].reshape(RW_HEADS, RW_HEAD))
    y, S_rw = rwkv7_scan(r_h, lw_h, k_h, v_h, kk, a_h, S_rw0.astype(f32))
    mean = jnp.mean(y, axis=-1, keepdims=True)
    var = jnp.mean(jnp.square(y - mean), axis=-1, keepdims=True)
    y = ((y - mean) * lax.rsqrt(var + RW_GN_EPS) * p['rw_ln_w'][l].reshape(RW_HEADS, RW_HEAD)
         + p['rw_ln_b'][l].reshape(RW_HEADS, RW_HEAD))
    y = y + jnp.sum(r_h * k_h * p['rw_rk'][l], axis=-1, keepdims=True) * v_h
    y = (y.reshape(B, L, RW_WIDTH) * g.astype(f32)).astype(h.dtype)

    out = jnp.concatenate([o_hg, y], axis=-1) @ p['w_out'][l]
    return out, S_hg, S_rw, v_first, h[:, -1, :]


def routed_experts(x, expert_ids, gates, wg, wu, wd):
    T, D = x.shape
    N = T * TOP_K
    flat_e = expert_ids.reshape(N)
    flat_tok = jnp.arange(N, dtype=jnp.int32) // TOP_K
    flat_g = gates.reshape(N)
    order = jnp.argsort(flat_e)
    sorted_e = flat_e[order]
    counts = jnp.bincount(flat_e, length=N_EXPERTS)
    padded = ((counts + MOE_BLOCK - 1) // MOE_BLOCK) * MOE_BLOCK
    start = jnp.cumsum(counts) - counts
    pstart = jnp.cumsum(padded) - padded
    dest = pstart[sorted_e] + jnp.arange(N, dtype=jnp.int32) - start[sorted_e]
    n_blocks = -(-(N + N_EXPERTS * (MOE_BLOCK - 1)) // MOE_BLOCK)
    P = n_blocks * MOE_BLOCK
    buf_tok = jnp.full((P,), T, jnp.int32).at[dest].set(flat_tok[order])
    buf_gate = jnp.zeros((P,), x.dtype).at[dest].set(flat_g[order].astype(x.dtype))
    block_end = jnp.cumsum(padded) // MOE_BLOCK
    block_e = jnp.minimum(jnp.searchsorted(block_end, jnp.arange(n_blocks), side='right'), N_EXPERTS - 1)
    x_pad = jnp.concatenate([x, jnp.zeros((1, D), x.dtype)], axis=0)

    def run_block(args):
        tok, gate, e = args
        xb = x_pad[tok]
        hdn = jax.nn.silu(xb @ wg[e]) * (xb @ wu[e])
        return (hdn @ wd[e]) * gate[:, None]

    yb = lax.map(run_block, (buf_tok.reshape(n_blocks, MOE_BLOCK),
                             buf_gate.reshape(n_blocks, MOE_BLOCK), block_e))
    y = jnp.zeros((T + 1, D), yb.dtype).at[buf_tok].add(yb.reshape(P, D))[:T]
    return y.astype(x.dtype)


def hier_moe(h, l, p):
    B, L, D = h.shape
    T = B * L
    x = h.reshape(T, D)
    f32 = jnp.float32
    g_logits = (x @ p['router_g_w'][l] + p['router_g_b'][l]).astype(f32)
    grp = jnp.argmax(g_logits, axis=-1).astype(jnp.int32)
    p_grp = jnp.take_along_axis(jax.nn.softmax(g_logits, axis=-1), grp[:, None], axis=-1)
    e_logits = (jnp.einsum('td,gde->tge', x, p['router_e_w'][l]) + p['router_e_b'][l]).astype(f32)
    e_logits = jnp.take_along_axis(e_logits, grp[:, None, None], axis=1)[:, 0]
    top_v, top_i = lax.top_k(e_logits, TOP_K)
    gates = p_grp * jax.nn.softmax(top_v, axis=-1)
    expert_ids = grp[:, None] * EXPERTS_PER_GROUP + top_i.astype(jnp.int32)
    y = routed_experts(x, expert_ids, gates, p['w_gate'][l], p['w_up'][l], p['w_down'][l])
    return y.reshape(B, L, D)


def trunk(x, c, S_hg, S_rw, h_prev, p):
    f32 = jnp.float32
    lb = jnp.cumsum(jax.nn.softmax(p['hg_lb_logits'].astype(f32), axis=0), axis=0)
    lb = lb - lb[0]
    cs = jax.nn.silu(c)
    new_hg, new_rw, new_shift = [], [], []
    v_first = None
    for l in range(DEPTH):
        mod = (cs @ p['w_ada'][l] + p['b_ada'][l])[:, None, :]
        sh1, sc1, g1, sh2, sc2, g2 = jnp.split(mod, 6, axis=-1)
        h = rmsnorm(x, p['norm_mix_w'][l]) * (1.0 + sc1) + sh1
        m, s_hg, s_rw, v_first, h_last = token_mixers(h, h_prev[l], S_hg[l], S_rw[l], v_first, l, lb, p)
        x = x + g1 * m
        h2 = rmsnorm(x, p['norm_ffn_w'][l]) * (1.0 + sc2) + sh2
        x = x + g2 * hier_moe(h2, l, p)
        new_hg.append(s_hg.astype(x.dtype))
        new_rw.append(s_rw.astype(x.dtype))
        new_shift.append(h_last.astype(x.dtype))
    y = rmsnorm(x, p['final_norm_w'])
    return y, jnp.stack(new_hg), jnp.stack(new_rw), jnp.stack(new_shift)


def setup_inputs(seed: int = 0) -> dict:
    key = jax.random.key(seed)
    ks = iter(jax.random.split(key, 48))
    f32 = jnp.float32

    def nrm(shape, scale):
        return jax.random.normal(next(ks), shape, f32) * scale

    def uni(shape, lo, hi):
        return jax.random.uniform(next(ks), shape, f32, lo, hi)

    D = D_MODEL
    inv = D ** -0.5
    return {
        'x_prompt': nrm((BATCH, SEQ, D), 1.0),
        'x_sample': nrm((DEC_BATCH, DEC_SEQ, D), 1.0),
        'state_hgrn': nrm((DEPTH, DEC_BATCH, HG_HEADS, HG_DK, HG_DV), 0.5),
        'state_rwkv': nrm((DEPTH, DEC_BATCH, RW_HEADS, RW_HEAD, RW_HEAD), 0.2),
        'state_shift': nrm((DEPTH, DEC_BATCH, D), 1.0),
        'c_prompt': nrm((BATCH, D), 1.0),
        'c_sample': nrm((DEC_BATCH, D), 1.0),
        'w_ada': nrm((DEPTH, D, 6 * D), 0.5 * inv),
        'b_ada': nrm((DEPTH, 6 * D), 0.02),
        'norm_mix_w': 1.0 + nrm((DEPTH, D), 0.05),
        'norm_ffn_w': 1.0 + nrm((DEPTH, D), 0.05),
        'w_in': nrm((DEPTH, D, IN_WIDTH), inv),
        'w_out': nrm((DEPTH, MIX_WIDTH, D), MIX_WIDTH ** -0.5),
        'hg_lb_logits': nrm((DEPTH, HG_HEADS * HG_DK), 1.0),
        'hg_norm_w': 1.0 + nrm((DEPTH, HG_WIDTH), 0.05),
        'rw_mu_proj': uni((DEPTH, RW_PROJ), 0.0, 1.0),
        'rw_mu_lora': uni((DEPTH, 3, D), 0.0, 1.0),
        'rw_w0': uni((DEPTH, RW_WIDTH), -6.0, -1.0),
        'rw_w1': nrm((DEPTH, D, RW_DECAY_LORA), inv),
        'rw_w2': nrm((DEPTH, RW_DECAY_LORA, RW_WIDTH), 0.5 * RW_DECAY_LORA ** -0.5),
        'rw_a0': nrm((DEPTH, RW_WIDTH), 0.1),
        'rw_a1': nrm((DEPTH, D, RW_ICLR_LORA), inv),
        'rw_a2': nrm((DEPTH, RW_ICLR_LORA, RW_WIDTH), 0.5 * RW_ICLR_LORA ** -0.5),
        'rw_g1': nrm((DEPTH, D, RW_GATE_LORA), inv),
        'rw_g2': nrm((DEPTH, RW_GATE_LORA, RW_WIDTH), RW_GATE_LORA ** -0.5),
        'rw_mu_vres': uni((DEPTH - 1, D), 0.0, 1.0),
        'rw_v0': nrm((DEPTH - 1, RW_WIDTH), 0.1),
        'rw_v1': nrm((DEPTH - 1, D, RW_VRES_LORA), inv),
        'rw_v2': nrm((DEPTH - 1, RW_VRES_LORA, RW_WIDTH), 0.5 * RW_VRES_LORA ** -0.5),
        'rw_kk': 0.85 + nrm((DEPTH, RW_WIDTH), 0.05),
        'rw_ka': 1.0 + nrm((DEPTH, RW_WIDTH), 0.05),
        'rw_rk': nrm((DEPTH, RW_HEADS, RW_HEAD), 0.1),
        'rw_ln_w': 1.0 + nrm((DEPTH, RW_WIDTH), 0.05),
        'rw_ln_b': nrm((DEPTH, RW_WIDTH), 0.02),
        'router_g_w': nrm((DEPTH, D, N_GROUPS), inv),
        'router_g_b': nrm((DEPTH, N_GROUPS), 0.01),
        'router_e_w': nrm((DEPTH, N_GROUPS, D, EXPERTS_PER_GROUP), inv),
        'router_e_b': nrm((DEPTH, N_GROUPS, EXPERTS_PER_GROUP), 0.01),
        'w_gate': nrm((DEPTH, N_EXPERTS, D, D_EXPERT), inv),
        'w_up': nrm((DEPTH, N_EXPERTS, D, D_EXPERT), inv),
        'w_down': nrm((DEPTH, N_EXPERTS, D_EXPERT, D), D_EXPERT ** -0.5),
        'final_norm_w': 1.0 + nrm((D,), 0.05),
    }


def reference(x_prompt, x_sample, state_hgrn, state_rwkv, state_shift, c_prompt, c_sample,
              w_ada, b_ada, norm_mix_w, norm_ffn_w, w_in, w_out, hg_lb_logits, hg_norm_w,
              rw_mu_proj, rw_mu_lora, rw_w0, rw_w1, rw_w2, rw_a0, rw_a1, rw_a2, rw_g1, rw_g2,
              rw_mu_vres, rw_v0, rw_v1, rw_v2, rw_kk, rw_ka, rw_rk, rw_ln_w, rw_ln_b,
              router_g_w, router_g_b, router_e_w, router_e_b, w_gate, w_up, w_down, final_norm_w):
    p = dict(w_ada=w_ada, b_ada=b_ada, norm_mix_w=norm_mix_w, norm_ffn_w=norm_ffn_w,
             w_in=w_in, w_out=w_out, hg_lb_logits=hg_lb_logits, hg_norm_w=hg_norm_w,
             rw_mu_proj=rw_mu_proj, rw_mu_lora=rw_mu_lora, rw_w0=rw_w0, rw_w1=rw_w1, rw_w2=rw_w2,
             rw_a0=rw_a0, rw_a1=rw_a1, rw_a2=rw_a2, rw_g1=rw_g1, rw_g2=rw_g2,
             rw_mu_vres=rw_mu_vres, rw_v0=rw_v0, rw_v1=rw_v1, rw_v2=rw_v2,
             rw_kk=rw_kk, rw_ka=rw_ka, rw_rk=rw_rk, rw_ln_w=rw_ln_w, rw_ln_b=rw_ln_b,
             router_g_w=router_g_w, router_g_b=router_g_b, router_e_w=router_e_w,
             router_e_b=router_e_b, w_gate=w_gate, w_up=w_up, w_down=w_down,
             final_norm_w=final_norm_w)
    B = x_prompt.shape[0]
    dt = x_prompt.dtype
    zero_hg = jnp.zeros((DEPTH, B) + state_hgrn.shape[2:], dt)
    zero_rw = jnp.zeros((DEPTH, B) + state_rwkv.shape[2:], dt)
    zero_sh = jnp.zeros((DEPTH, B) + state_shift.shape[2:], dt)
    y_prompt, hg_p, rw_p, sh_p = trunk(x_prompt, c_prompt, zero_hg, zero_rw, zero_sh, p)
    y_sample, hg_s, rw_s, sh_s = trunk(x_sample, c_sample, state_hgrn, state_rwkv, state_shift, p)
    return (y_prompt, y_sample, hg_p, rw_p, sh_p, hg_s, rw_s, sh_s)
```

```python
import functools

import jax
import jax.numpy as jnp
from jax import lax
from jax.experimental import pallas as pl
from jax.experimental.pallas import tpu as pltpu

F32 = jnp.float32
BF16 = jnp.bfloat16

D_MODEL = 1024
HG_HEADS = 4
HG_DK = 128
HG_DV = 128
HG_WIDTH = HG_HEADS * HG_DV
RW_HEADS = 8
RW_HEAD = 64
RW_WIDTH = RW_HEADS * RW_HEAD
HG_PROJ = 2 * HG_HEADS * HG_DK + 2 * HG_WIDTH
RW_PROJ = 3 * RW_WIDTH
IN_WIDTH = HG_PROJ + RW_PROJ
N_GROUPS = 4
EXPERTS_PER_GROUP = 8
N_EXPERTS = N_GROUPS * EXPERTS_PER_GROUP
TOP_K = 2
D_EXPERT = 512
NORM_EPS = 1e-6
RW_GN_EPS = 64e-5

CHUNK = 64
SUB = 16
ROUTE_W = 128
RW_PACK = 7 * RW_WIDTH
VMEM_LIMIT = 56 * 1024 * 1024

NN = (((1,), (0,)), ((), ()))
NT = (((1,), (1,)), ((), ()))
TN = (((0,), (0,)), ((), ()))


def _split2(a):
    hi = a.astype(BF16)
    lo = (a - hi.astype(F32)).astype(BF16)
    return hi, lo


def _mm(a, b, dims=NN, passes=1):
    if passes == 1:
        return lax.dot_general(a.astype(BF16), b.astype(BF16), dims, preferred_element_type=F32)
    a1, a2 = _split2(a)
    b1, b2 = _split2(b)
    out = lax.dot_general(a1, b1, dims, preferred_element_type=F32)
    out = out + lax.dot_general(a1, b2, dims, preferred_element_type=F32)
    return out + lax.dot_general(a2, b1, dims, preferred_element_type=F32)


def _mm_exact_lhs(a_bf, b):
    b1 = b.astype(BF16)
    r = b - b1.astype(F32)
    b2 = r.astype(BF16)
    b3 = (r - b2.astype(F32)).astype(BF16)
    out = jnp.dot(a_bf, b1, preferred_element_type=F32)
    out = out + jnp.dot(a_bf, b2, preferred_element_type=F32)
    return out + jnp.dot(a_bf, b3, preferred_element_type=F32)


def _sigmoid(x):
    return 1.0 / (1.0 + jnp.exp(-x))


def _silu(x):
    return x * _sigmoid(x)


def _softplus(x):
    return jnp.maximum(x, 0.0) + jnp.log1p(jnp.exp(-jnp.abs(x)))


def _tri_incl_bf16(n):
    r = lax.broadcasted_iota(jnp.int32, (n, n), 0)
    c = lax.broadcasted_iota(jnp.int32, (n, n), 1)
    return (r >= c).astype(BF16)


def _params(sem):
    return pltpu.CompilerParams(dimension_semantics=sem, vmem_limit_bytes=VMEM_LIMIT)


def _ada_kernel(c_ref, w_ref, b_ref, o_ref):
    cs = _silu(c_ref[...])
    o_ref[...] = _mm(cs, w_ref[...]) + b_ref[...]


def _ada_mod(c_all, w_ada, b_ada):
    depth, d, n = w_ada.shape
    rows = c_all.shape[0]
    tn = 1536
    return pl.pallas_call(
        _ada_kernel,
        grid=(depth, n // tn),
        in_specs=[
            pl.BlockSpec((rows, d), lambda l, j: (0, 0)),
            pl.BlockSpec((None, d, tn), lambda l, j: (l, 0, j)),
            pl.BlockSpec((None, 1, tn), lambda l, j: (l, 0, j)),
        ],
        out_specs=pl.BlockSpec((None, rows, tn), lambda l, j: (l, 0, j)),
        out_shape=jax.ShapeDtypeStruct((depth, rows, n), F32),
        compiler_params=_params(("parallel", "parallel")),
        name="ada_mod",
    )(c_all, w_ada, b_ada.reshape(depth, 1, n))


def _premix_kernel(has_vres, *refs):
    if has_vres:
        (x_ref, bvec_ref, vecd_ref, vec5_ref, mup_ref, win_ref, w1_ref, a1_ref, g1_ref, v1_ref,
         w2_ref, a2_ref, g2_ref, v2_ref, seg_ref, vfirst_ref,
         hg_ref, rw_ref, hlast_ref, hcar, pcar) = refs
    else:
        (x_ref, bvec_ref, vecd_ref, vec5_ref, mup_ref, win_ref, w1_ref, a1_ref, g1_ref,
         w2_ref, a2_ref, g2_ref, seg_ref,
         hg_ref, rw_ref, hlast_ref, hcar, pcar) = refs
    t = pl.program_id(1)
    tm = x_ref.shape[0]
    w_rw = win_ref[:, HG_PROJ:]

    @pl.when(t == 0)
    def _():
        hp = jnp.broadcast_to(bvec_ref[2:3, :], (8, D_MODEL))
        hcar[...] = hp
        pcar[...] = _mm(hp, w_rw)

    x = x_ref[...]
    nw = vecd_ref[0:1, :]
    sh = bvec_ref[0:1, :]
    sc = bvec_ref[1:2, :]
    xn = x * lax.rsqrt(jnp.mean(x * x, axis=-1, keepdims=True) + NORM_EPS) * nw
    h = xn * (1.0 + sc) + sh

    hb = h.astype(BF16)
    hg_ref[...] = jnp.dot(hb, win_ref[:, :HG_PROJ], preferred_element_type=F32)
    p_cur = jnp.dot(hb, w_rw, preferred_element_type=F32)

    first = lax.broadcasted_iota(jnp.int32, (tm, 1), 0) == 0
    h_shift = jnp.where(first, hcar[0:1, :], pltpu.roll(h, 1, 0))
    p_shift = jnp.where(first, pcar[0:1, :], pltpu.roll(p_cur, 1, 0))
    hcar[0:1, :] = h[tm - 1:tm, :]
    pcar[0:1, :] = p_cur[tm - 1:tm, :]
    hlast_ref[...] = h[tm - 1:tm, :]

    pm = p_cur + (p_shift - p_cur) * mup_ref[...]
    r = pm[:, :RW_WIDTH]
    k = pm[:, RW_WIDTH:2 * RW_WIDTH]
    v = pm[:, 2 * RW_WIDTH:]

    dh = h_shift - h
    xw = h + dh * vecd_ref[1:2, :]
    xa = h + dh * vecd_ref[2:3, :]
    xg = h + dh * vecd_ref[3:4, :]
    w_pre = vec5_ref[0:1, :] + _mm(jnp.tanh(_mm(xw, w1_ref[...])), w2_ref[...])
    logw = -jnp.exp(-_softplus(-w_pre) - 0.5)
    a = _sigmoid(vec5_ref[1:2, :] + _mm(_mm(xa, a1_ref[...]), a2_ref[...]))
    g = _mm(_sigmoid(_mm(xg, g1_ref[...])), g2_ref[...])
    if has_vres:
        xv = h + dh * vecd_ref[4:5, :]
        nu = _sigmoid(vec5_ref[2:3, :] + _mm(_mm(xv, v1_ref[...]), v2_ref[...]))
        v = v + (vfirst_ref[...] - v) * nu

    kk = k * vec5_ref[3:4, :]
    ss = _mm(kk * kk, seg_ref[...], passes=3)
    kappa = kk / jnp.maximum(jnp.sqrt(ss), 1e-12)
    k_mod = k * (1.0 + (a - 1.0) * vec5_ref[4:5, :])

    rw_ref[:, 0 * RW_WIDTH:1 * RW_WIDTH] = r
    rw_ref[:, 1 * RW_WIDTH:2 * RW_WIDTH] = logw
    rw_ref[:, 2 * RW_WIDTH:3 * RW_WIDTH] = k_mod
    rw_ref[:, 3 * RW_WIDTH:4 * RW_WIDTH] = v
    rw_ref[:, 4 * RW_WIDTH:5 * RW_WIDTH] = kappa
    rw_ref[:, 5 * RW_WIDTH:6 * RW_WIDTH] = kappa * a
    rw_ref[:, 6 * RW_WIDTH:7 * RW_WIDTH] = g


def _premix(x, bvec, vecd, vec5, mup, win, loras, seg, vfirst, tm):
    b, l, d = x.shape
    has_vres = vfirst is not None
    const = lambda shape: pl.BlockSpec(shape, lambda i, j: tuple(0 for _ in shape))
    in_specs = [
        pl.BlockSpec((None, tm, d), lambda i, j: (i, j, 0)),
        pl.BlockSpec((None, 8, d), lambda i, j: (i, 0, 0)),
        const(vecd.shape), const(vec5.shape), const(mup.shape), const(win.shape),
    ]
    in_specs += [const(w.shape) for w in loras]
    in_specs.append(const(seg.shape))
    args = [x, bvec, vecd, vec5, mup, win, *loras, seg]
    if has_vres:
        in_specs.append(pl.BlockSpec((None, tm, RW_WIDTH), lambda i, j: (i, j, 3)))
        args.append(vfirst)
    return pl.pallas_call(
        functools.partial(_premix_kernel, has_vres),
        grid=(b, l // tm),
        in_specs=in_specs,
        out_specs=[
            pl.BlockSpec((None, tm, HG_PROJ), lambda i, j: (i, j, 0)),
            pl.BlockSpec((None, tm, RW_PACK), lambda i, j: (i, j, 0)),
            pl.BlockSpec((None, 1, d), lambda i, j: (i, 0, 0)),
        ],
        out_shape=[
            jax.ShapeDtypeStruct((b, l, HG_PROJ), F32),
            jax.ShapeDtypeStruct((b, l, RW_PACK), F32),
            jax.ShapeDtypeStruct((b, 1, d), F32),
        ],
        scratch_shapes=[pltpu.VMEM((8, d), F32), pltpu.VMEM((8, RW_PROJ), F32)],
        compiler_params=_params(("parallel", "arbitrary")),
        name="premix",
    )(*args)


def _hgrn_chunk(zq, zf, zi, zg, st, loglb, log1mlb, nw, tri):
    c = zq.shape[0]
    q = _silu(zq)
    bterm = log1mlb + (jnp.minimum(zf, 0.0) - jnp.log1p(jnp.exp(-jnp.abs(zf))))
    logf = jnp.maximum(loglb, bterm) + jnp.log1p(jnp.exp(-jnp.abs(loglb - bterm)))
    k = 1.0 - jnp.exp(logf)
    v = zi
    bcum = _mm_exact_lhs(tri, logf)
    rows = lax.broadcasted_iota(jnp.int32, (SUB, 1), 0)
    outs = []
    for i in range(c // SUB):
        lo = i * SUB
        bi = bcum[lo:lo + SUB]
        qi = q[lo:lo + SUB]
        ki = k[lo:lo + SUB]
        vi = v[lo:lo + SUB]
        o = _mm(qi * jnp.exp(bi), st, NT)
        if i > 0:
            b0 = bcum[lo - 1:lo]
            qt = qi * jnp.exp(bi - b0)
            kt = k[:lo] * jnp.exp(b0 - bcum[:lo])
            o = o + _mm(_mm(qt, kt, NT), v[:lo])
        for s in range(SUB):
            dec = jnp.exp(jnp.minimum(bi - bi[s:s + 1], 0.0))
            col = jnp.sum(qi * ki[s:s + 1] * dec, axis=-1, keepdims=True)
            o = o + jnp.where(rows >= s, col, 0.0) * vi[s:s + 1]
        outs.append(o)
    o = outs[0] if len(outs) == 1 else jnp.concatenate(outs, axis=0)
    b_last = bcum[c - 1:c]
    st_new = st * jnp.exp(b_last) + _mm(v, k * jnp.exp(b_last - bcum), TN)
    on = o * lax.rsqrt(jnp.mean(o * o, axis=-1, keepdims=True) + NORM_EPS) * nw
    return on * _silu(zg), st_new


def _hgrn_kernel(c, hg_ref, lb_ref, nw_ref, s0_ref, o_ref, sfin_ref):
    t = pl.program_id(1)

    @pl.when(t == 0)
    def _():
        sfin_ref[...] = s0_ref[...]

    tri = _tri_incl_bf16(c)
    qk = HG_HEADS * HG_DK
    for ci in range(hg_ref.shape[0] // c):
        r0 = ci * c
        for h in range(HG_HEADS):
            zq = hg_ref[r0:r0 + c, h * HG_DK:(h + 1) * HG_DK]
            zf = hg_ref[r0:r0 + c, qk + h * HG_DK:qk + (h + 1) * HG_DK]
            zi = hg_ref[r0:r0 + c, 2 * qk + h * HG_DV:2 * qk + (h + 1) * HG_DV]
            zg = hg_ref[r0:r0 + c, 2 * qk + HG_WIDTH + h * HG_DV:2 * qk + HG_WIDTH + (h + 1) * HG_DV]
            sl = slice(h * HG_DK, (h + 1) * HG_DK)
            out, st_new = _hgrn_chunk(zq, zf, zi, zg, sfin_ref[h], lb_ref[0:1, sl], lb_ref[1:2, sl],
                                      nw_ref[0:1, h * HG_DV:(h + 1) * HG_DV], tri)
            sfin_ref[h] = st_new
            o_ref[r0:r0 + c, h * HG_DV:(h + 1) * HG_DV] = out


def _hgrn(hgraw, lbvec, nw, s0t, ch, c):
    b, l, _ = hgraw.shape
    return pl.pallas_call(
        functools.partial(_hgrn_kernel, c),
        grid=(b, l // ch),
        in_specs=[
            pl.BlockSpec((None, ch, HG_PROJ), lambda i, j: (i, j, 0)),
            pl.BlockSpec(lbvec.shape, lambda i, j: (0, 0)),
            pl.BlockSpec(nw.shape, lambda i, j: (0, 0)),
            pl.BlockSpec((None, HG_HEADS, HG_DV, HG_DK), lambda i, j: (i, 0, 0, 0)),
        ],
        out_specs=[
            pl.BlockSpec((None, ch, HG_WIDTH), lambda i, j: (i, j, 0)),
            pl.BlockSpec((None, HG_HEADS, HG_DV, HG_DK), lambda i, j: (i, 0, 0, 0)),
        ],
        out_shape=[
            jax.ShapeDtypeStruct((b, l, HG_WIDTH), F32),
            jax.ShapeDtypeStruct((b, HG_HEADS, HG_DV, HG_DK), F32),
        ],
        compiler_params=_params(("parallel", "arbitrary")),
        name="hgrn",
    )(hgraw, lbvec, nw, s0t)


def _rwkv_solve_neg(aab, w, same_sub):
    c = aab.shape[0]
    d1 = jnp.where(same_sub, aab, 0.0)
    d2 = _mm(d1, d1, passes=3)
    d4 = _mm(d2, d2, passes=3)
    d8 = _mm(d4, d4, passes=3)
    z = w if c == SUB else jnp.concatenate([aab - d1, w], axis=1)
    z = z - _mm(d1, z, passes=3)
    z = z + _mm(d2, z, passes=3)
    z = z + _mm(d4, z, passes=3)
    z = z + _mm(d8, z, passes=3)
    if c == SUB:
        return -z
    f = z[:, :c]
    x = z[:, c:]
    x1 = x + _mm(f, _mm(f, x, passes=3), passes=3)
    return _mm(f, x1, passes=3) - x1


def _rwkv_kernel(c, rw_ref, vec_ref, s0_ref, o_ref, sfin_ref):
    t = pl.program_id(1)

    @pl.when(t == 0)
    def _():
        sfin_ref[...] = s0_ref[...]

    tri = _tri_incl_bf16(c)
    row = lax.broadcasted_iota(jnp.int32, (c, c), 0)
    col = lax.broadcasted_iota(jnp.int32, (c, c), 1)
    strict = row > col
    incl = row >= col
    same_sub = (row // SUB) == (col // SUB)
    w = RW_WIDTH
    for ci in range(rw_ref.shape[0] // c):
        r0 = ci * c
        r = rw_ref[r0:r0 + c, 0 * w:1 * w]
        lw = rw_ref[r0:r0 + c, 1 * w:2 * w]
        km = rw_ref[r0:r0 + c, 2 * w:3 * w]
        v = rw_ref[r0:r0 + c, 3 * w:4 * w]
        kap = rw_ref[r0:r0 + c, 4 * w:5 * w]
        beta = rw_ref[r0:r0 + c, 5 * w:6 * w]
        gate = rw_ref[r0:r0 + c, 6 * w:7 * w]
        cum = _mm_exact_lhs(tri, lw)
        c_last = cum[c - 1:c]
        e_neg = jnp.exp(-cum)
        e_hat = jnp.exp(c_last - cum)
        kbar = kap * jnp.exp(cum - lw)
        rbar = r * jnp.exp(cum)
        bneg = beta * e_neg
        kneg = km * e_neg
        bhat = beta * e_hat
        khat = km * e_hat
        e_last = jnp.exp(c_last)
        bonus = r * km * vec_ref[2:3, :]
        for h in range(RW_HEADS):
            sl = slice(h * RW_HEAD, (h + 1) * RW_HEAD)
            s_h = sfin_ref[h]
            v_h = v[:, sl]
            lhs = jnp.concatenate([kbar[:, sl], rbar[:, sl]], axis=0)
            rhs = jnp.concatenate([bneg[:, sl], kneg[:, sl]], axis=0)
            gmat = _mm(lhs, rhs, NT, passes=3)
            p0 = _mm(lhs, s_h, NT, passes=3)
            aab = jnp.where(strict, gmat[:c, :c], 0.0)
            aak = jnp.where(strict, gmat[:c, c:], 0.0)
            aqb = jnp.where(incl, gmat[c:, :c], 0.0)
            aqk = jnp.where(incl, gmat[c:, c:], 0.0)
            wmat = p0[:c] + _mm(aak, v_h, passes=3)
            u = _rwkv_solve_neg(aab, wmat, same_sub)
            y = p0[c:] + _mm(aqb, u, passes=3) + _mm(aqk, v_h, passes=3)
            s_new = (s_h * e_last[:, sl] + _mm(u, bhat[:, sl], TN, passes=3)
                     + _mm(v_h, khat[:, sl], TN, passes=3))
            sfin_ref[h] = s_new
            mean = jnp.mean(y, axis=-1, keepdims=True)
            yc = y - mean
            var = jnp.mean(yc * yc, axis=-1, keepdims=True)
            yn = yc * lax.rsqrt(var + RW_GN_EPS) * vec_ref[0:1, sl] + vec_ref[1:2, sl]
            yn = yn + jnp.sum(bonus[:, sl], axis=-1, keepdims=True) * v_h
            o_ref[r0:r0 + c, sl] = yn * gate[:, sl]


def _rwkv(rwp, vec, s0, ch, c):
    b, l, _ = rwp.shape
    return pl.pallas_call(
        functools.partial(_rwkv_kernel, c),
        grid=(b, l // ch),
        in_specs=[
            pl.BlockSpec((None, ch, RW_PACK), lambda i, j: (i, j, 0)),
            pl.BlockSpec(vec.shape, lambda i, j: (0, 0)),
            pl.BlockSpec((None, RW_HEADS, RW_HEAD, RW_HEAD), lambda i, j: (i, 0, 0, 0)),
        ],
        out_specs=[
            pl.BlockSpec((None, ch, RW_WIDTH), lambda i, j: (i, j, 0)),
            pl.BlockSpec((None, RW_HEADS, RW_HEAD, RW_HEAD), lambda i, j: (i, 0, 0, 0)),
        ],
        out_shape=[
            jax.ShapeDtypeStruct((b, l, RW_WIDTH), F32),
            jax.ShapeDtypeStruct((b, RW_HEADS, RW_HEAD, RW_HEAD), F32),
        ],
        compiler_params=_params(("parallel", "arbitrary")),
        name="rwkv",
    )(rwp, vec, s0)


def _postmix_kernel(ohg_ref, yrw_ref, x_ref, bvec_ref, nw_ref, wout_ref, wr_ref, br_ref,
                    xn_ref, h2_ref, route_ref):
    m = (jnp.dot(ohg_ref[...].astype(BF16), wout_ref[:HG_WIDTH, :], preferred_element_type=F32)
         + jnp.dot(yrw_ref[...].astype(BF16), wout_ref[HG_WIDTH:, :], preferred_element_type=F32))
    xn = x_ref[...] + bvec_ref[0:1, :] * m
    xn_ref[...] = xn
    h2 = (xn * lax.rsqrt(jnp.mean(xn * xn, axis=-1, keepdims=True) + NORM_EPS) * nw_ref[...]
          * (1.0 + bvec_ref[2:3, :]) + bvec_ref[1:2, :])
    h2_ref[...] = h2

    logits = _mm(h2, wr_ref[...]) + br_ref[...]
    lane = lax.broadcasted_iota(jnp.int32, logits.shape, 1)
    neg = jnp.float32(-jnp.inf)
    big = jnp.int32(ROUTE_W)
    is_g = lane < N_GROUPS
    gl = jnp.where(is_g, logits, neg)
    gmax = jnp.max(gl, axis=-1, keepdims=True)
    grp = jnp.min(jnp.where(gl == gmax, lane, big), axis=-1, keepdims=True)
    p_grp = 1.0 / jnp.sum(jnp.where(is_g, jnp.exp(gl - gmax), 0.0), axis=-1, keepdims=True)
    e_lo = N_GROUPS + grp * EXPERTS_PER_GROUP
    in_grp = (lane >= e_lo) & (lane < e_lo + EXPERTS_PER_GROUP)
    el = jnp.where(in_grp, logits, neg)
    m1 = jnp.max(el, axis=-1, keepdims=True)
    i1 = jnp.min(jnp.where(el == m1, lane, big), axis=-1, keepdims=True)
    el2 = jnp.where(lane == i1, neg, el)
    m2 = jnp.max(el2, axis=-1, keepdims=True)
    i2 = jnp.min(jnp.where(el2 == m2, lane, big), axis=-1, keepdims=True)
    e21 = jnp.exp(m2 - m1)
    p1 = 1.0 / (1.0 + e21)
    p2 = e21 / (1.0 + e21)
    out = jnp.where(lane == 0, (i1 - N_GROUPS).astype(F32),
                    jnp.where(lane == 1, (i2 - N_GROUPS).astype(F32),
                              jnp.where(lane == 2, p_grp * p1,
                                        jnp.where(lane == 3, p_grp * p2, 0.0))))
    route_ref[...] = out[:, :8]


def _postmix(ohg, yrw, x, bvec, nw, wout, wr, br, tm):
    b, l, d = x.shape
    const = lambda shape: pl.BlockSpec(shape, lambda i, j: tuple(0 for _ in shape))
    return pl.pallas_call(
        _postmix_kernel,
        grid=(b, l // tm),
        in_specs=[
            pl.BlockSpec((None, tm, HG_WIDTH), lambda i, j: (i, j, 0)),
            pl.BlockSpec((None, tm, RW_WIDTH), lambda i, j: (i, j, 0)),
            pl.BlockSpec((None, tm, d), lambda i, j: (i, j, 0)),
            pl.BlockSpec((None, 8, d), lambda i, j: (i, 0, 0)),
            const(nw.shape), const(wout.shape), const(wr.shape), const(br.shape),
        ],
        out_specs=[
            pl.BlockSpec((None, tm, d), lambda i, j: (i, j, 0)),
            pl.BlockSpec((None, tm, d), lambda i, j: (i, j, 0)),
            pl.BlockSpec((None, tm, 8), lambda i, j: (i, j, 0)),
        ],
        out_shape=[
            jax.ShapeDtypeStruct((b, l, d), F32),
            jax.ShapeDtypeStruct((b, l, d), F32),
            jax.ShapeDtypeStruct((b, l, 8), F32),
        ],
        compiler_params=_params(("parallel", "parallel")),
        name="postmix",
    )(ohg, yrw, x, bvec, nw, wout, wr, br)


def _expert_kernel(be_ref, nu_ref, xs_ref, wg_ref, wu_ref, wd_ref, o_ref):
    i = pl.program_id(0)

    @pl.when(i < nu_ref[0])
    def _():
        xb = xs_ref[...].astype(BF16)
        hg = jnp.dot(xb, wg_ref[...], preferred_element_type=F32)
        hu = jnp.dot(xb, wu_ref[...], preferred_element_type=F32)
        hdn = (_silu(hg) * hu).astype(BF16)
        o_ref[...] = jnp.dot(hdn, wd_ref[...], preferred_element_type=F32)

    @pl.when(i >= nu_ref[0])
    def _():
        o_ref[...] = jnp.zeros_like(o_ref)


def _experts(xs, block_e, n_used, wg, wu, wd, blk):
    p, d = xs.shape
    nb = p // blk
    grid_spec = pltpu.PrefetchScalarGridSpec(
        num_scalar_prefetch=2,
        grid=(nb,),
        in_specs=[
            pl.BlockSpec((blk, d), lambda i, be, nu: (i, 0)),
            pl.BlockSpec((None, d, D_EXPERT), lambda i, be, nu: (be[i], 0, 0)),
            pl.BlockSpec((None, d, D_EXPERT), lambda i, be, nu: (be[i], 0, 0)),
            pl.BlockSpec((None, D_EXPERT, d), lambda i, be, nu: (be[i], 0, 0)),
        ],
        out_specs=pl.BlockSpec((blk, d), lambda i, be, nu: (i, 0)),
    )
    return pl.pallas_call(
        _expert_kernel,
        grid_spec=grid_spec,
        out_shape=jax.ShapeDtypeStruct((p, d), F32),
        compiler_params=_params(("arbitrary",)),
        name="experts",
    )(block_e, n_used, xs, wg, wu, wd)


def _combine_kernel(final, xn_ref, y0_ref, y1_ref, route_ref, bvec_ref, fw_ref, o_ref):
    rt = route_ref[...]
    moe = rt[:, 2:3] * y0_ref[...] + rt[:, 3:4] * y1_ref[...]
    x = xn_ref[...] + bvec_ref[3:4, :] * moe
    if final:
        x = x * lax.rsqrt(jnp.mean(x * x, axis=-1, keepdims=True) + NORM_EPS) * fw_ref[...]
    o_ref[...] = x


def _combine(xn, y0, y1, route, bvec, fw, final, tm):
    b, l, d = xn.shape
    row = lambda w: pl.BlockSpec((None, tm, w), lambda i, j: (i, j, 0))
    return pl.pallas_call(
        functools.partial(_combine_kernel, final),
        grid=(b, l // tm),
        in_specs=[row(d), row(d), row(d), row(8),
                  pl.BlockSpec((None, 8, d), lambda i, j: (i, 0, 0)),
                  pl.BlockSpec((1, d), lambda i, j: (0, 0))],
        out_specs=row(d),
        out_shape=jax.ShapeDtypeStruct((b, l, d), F32),
        compiler_params=_params(("parallel", "parallel")),
        name="combine",
    )(xn, y0, y1, route, bvec, fw)


def _dispatch(route, blk):
    t = route.shape[0]
    n = t * TOP_K
    flat_e = route[:, :TOP_K].astype(jnp.int32).reshape(n)
    order = jnp.argsort(flat_e)
    sorted_e = flat_e[order]
    counts = jnp.bincount(flat_e, length=N_EXPERTS)
    padded = ((counts + blk - 1) // blk) * blk
    start = jnp.cumsum(counts) - counts
    pend = jnp.cumsum(padded)
    pstart = pend - padded
    dest = pstart[sorted_e] + jnp.arange(n, dtype=jnp.int32) - start[sorted_e]
    nb = -(-(n + N_EXPERTS * (blk - 1)) // blk)
    buf_tok = jnp.zeros((nb * blk,), jnp.int32).at[dest].set((order // TOP_K).astype(jnp.int32))
    slot = jnp.zeros((n,), jnp.int32).at[order].set(dest.astype(jnp.int32))
    block_e = jnp.minimum(jnp.searchsorted(pend // blk, jnp.arange(nb), side='right'),
                          N_EXPERTS - 1).astype(jnp.int32)
    n_used = (pend[-1] // blk).astype(jnp.int32).reshape(1)
    return buf_tok, slot.reshape(t, TOP_K), block_e, n_used


def _layer_weights(p, l):
    f = lambda name: p[name][l]
    zero_d = jnp.zeros((D_MODEL,), F32)
    has_vres = l > 0
    vecd = jnp.stack([f('norm_mix_w'), p['rw_mu_lora'][l, 0], p['rw_mu_lora'][l, 1], p['rw_mu_lora'][l, 2],
                      p['rw_mu_vres'][l - 1] if has_vres else zero_d, zero_d, zero_d, zero_d])
    zero_w = jnp.zeros((RW_WIDTH,), F32)
    vec5 = jnp.stack([f('rw_w0'), f('rw_a0'), p['rw_v0'][l - 1] if has_vres else zero_w,
                      f('rw_kk'), f('rw_ka'), zero_w, zero_w, zero_w])
    loras = [f('rw_w1'), f('rw_a1'), f('rw_g1')] + ([p['rw_v1'][l - 1]] if has_vres else [])
    loras += [f('rw_w2'), f('rw_a2'), f('rw_g2')] + ([p['rw_v2'][l - 1]] if has_vres else [])
    loras = [w.astype(BF16) for w in loras]
    rwvec = jnp.stack([f('rw_ln_w'), f('rw_ln_b'), f('rw_rk').reshape(RW_WIDTH),
                       zero_w, zero_w, zero_w, zero_w, zero_w])
    wr = jnp.concatenate([f('router_g_w'),
                          jnp.transpose(f('router_e_w'), (1, 0, 2)).reshape(D_MODEL, N_EXPERTS),
                          jnp.zeros((D_MODEL, ROUTE_W - N_GROUPS - N_EXPERTS), F32)], axis=1)
    br = jnp.concatenate([f('router_g_b'), f('router_e_b').reshape(N_EXPERTS),
                          jnp.zeros((ROUTE_W - N_GROUPS - N_EXPERTS,), F32)]).reshape(1, ROUTE_W)
    return dict(
        vecd=vecd, vec5=vec5, mup=f('rw_mu_proj').reshape(1, RW_PROJ), win=f('w_in').astype(BF16),
        loras=loras, rwvec=rwvec, hg_nw=f('hg_norm_w').reshape(1, HG_WIDTH),
        nfw=f('norm_ffn_w').reshape(1, D_MODEL), wout=f('w_out').astype(BF16), wr=wr, br=br,
        wg=f('w_gate').astype(BF16), wu=f('w_up').astype(BF16), wd=f('w_down').astype(BF16))


def _trunk(x, mod, s_hg, s_rw, h_prev, lw_all, lbvecs, final_w):
    b, l, d = x.shape
    depth = len(lw_all)
    tm = min(l, 256)
    ch = min(l, 256)
    c = min(l, CHUNK)
    blk = 256 if b * l * TOP_K >= 8192 else 32
    head = lax.broadcasted_iota(jnp.int32, (RW_WIDTH, RW_WIDTH), 0) // RW_HEAD
    seg = (head == head.T).astype(BF16)
    new_hg, new_rw, new_shift = [], [], []
    rwp_first = None
    zrow = jnp.zeros((b, d), F32)
    for li in range(depth):
        w = lw_all[li]
        sh1, sc1, g1, sh2, sc2, g2 = jnp.split(mod[li], 6, axis=-1)
        bvec1 = jnp.stack([sh1, sc1, h_prev[li], zrow, zrow, zrow, zrow, zrow], axis=1)
        bvec2 = jnp.stack([g1, sh2, sc2, g2, zrow, zrow, zrow, zrow], axis=1)
        hgraw, rwp, hlast = _premix(x, bvec1, w['vecd'], w['vec5'], w['mup'], w['win'], w['loras'],
                                    seg, rwp_first if li > 0 else None, tm)
        if li == 0:
            rwp_first = rwp
        ohg, hg_t = _hgrn(hgraw, lbvecs[li], w['hg_nw'], jnp.swapaxes(s_hg[li], -1, -2), ch, c)
        yrw, rw_fin = _rwkv(rwp, w['rwvec'], s_rw[li], ch, c)
        xn, h2, route = _postmix(ohg, yrw, x, bvec2, w['nfw'], w['wout'], w['wr'], w['br'], tm)
        t = b * l
        buf_tok, slot, block_e, n_used = _dispatch(route.reshape(t, 8), blk)
        xs = jnp.take(h2.reshape(t, d), buf_tok, axis=0)
        yb = _experts(xs, block_e, n_used, w['wg'], w['wu'], w['wd'], blk)
        y0 = jnp.take(yb, slot[:, 0], axis=0).reshape(b, l, d)
        y1 = jnp.take(yb, slot[:, 1], axis=0).reshape(b, l, d)
        x = _combine(xn, y0, y1, route, bvec2, final_w, li == depth - 1, tm)
        new_hg.append(jnp.swapaxes(hg_t, -1, -2))
        new_rw.append(rw_fin)
        new_shift.append(hlast.reshape(b, d))
    return x, jnp.stack(new_hg), jnp.stack(new_rw), jnp.stack(new_shift)


def kernel(x_prompt, x_sample, state_hgrn, state_rwkv, state_shift, c_prompt, c_sample, w_ada, b_ada, norm_mix_w, norm_ffn_w, w_in, w_out, hg_lb_logits, hg_norm_w, rw_mu_proj, rw_mu_lora, rw_w0, rw_w1, rw_w2, rw_a0, rw_a1, rw_a2, rw_g1, rw_g2, rw_mu_vres, rw_v0, rw_v1, rw_v2, rw_kk, rw_ka, rw_rk, rw_ln_w, rw_ln_b, router_g_w, router_g_b, router_e_w, router_e_b, w_gate, w_up, w_down, final_norm_w):
    p = dict(norm_mix_w=norm_mix_w, norm_ffn_w=norm_ffn_w, w_in=w_in, w_out=w_out, hg_norm_w=hg_norm_w,
             rw_mu_proj=rw_mu_proj, rw_mu_lora=rw_mu_lora, rw_w0=rw_w0, rw_w1=rw_w1, rw_w2=rw_w2,
             rw_a0=rw_a0, rw_a1=rw_a1, rw_a2=rw_a2, rw_g1=rw_g1, rw_g2=rw_g2,
             rw_mu_vres=rw_mu_vres, rw_v0=rw_v0, rw_v1=rw_v1, rw_v2=rw_v2,
             rw_kk=rw_kk, rw_ka=rw_ka, rw_rk=rw_rk, rw_ln_w=rw_ln_w, rw_ln_b=rw_ln_b,
             router_g_w=router_g_w, router_g_b=router_g_b, router_e_w=router_e_w,
             router_e_b=router_e_b, w_gate=w_gate, w_up=w_up, w_down=w_down)
    depth = w_in.shape[0]
    bp = x_prompt.shape[0]
    bs = x_sample.shape[0]
    d = x_prompt.shape[-1]
    lw_all = [_layer_weights(p, l) for l in range(depth)]
    lb = jnp.cumsum(jax.nn.softmax(hg_lb_logits.astype(F32), axis=0), axis=0)
    lb = lb - lb[0]
    zpad = jnp.zeros((6, lb.shape[1]), F32)
    lbvecs = [jnp.concatenate([jnp.log(lb[l])[None], jnp.log1p(-lb[l])[None], zpad], axis=0)
              for l in range(depth)]
    rows = -(-(bp + bs) // 8) * 8
    c_all = jnp.concatenate([c_prompt, c_sample, jnp.zeros((rows - bp - bs, d), F32)], axis=0)
    mod = _ada_mod(c_all, w_ada, b_ada)
    fw = final_norm_w.reshape(1, d)
    zero_hg = jnp.zeros((depth, bp) + state_hgrn.shape[2:], F32)
    zero_rw = jnp.zeros((depth, bp) + state_rwkv.shape[2:], F32)
    zero_sh = jnp.zeros((depth, bp, d), F32)
    y_p, hg_p, rw_p, sh_p = _trunk(x_prompt, mod[:, :bp], zero_hg, zero_rw, zero_sh, lw_all, lbvecs, fw)
    y_s, hg_s, rw_s, sh_s = _trunk(x_sample, mod[:, bp:bp + bs], state_hgrn, state_rwkv, state_shift,
                                   lw_all, lbvecs, fw)
    return (y_p, y_s, hg_p, rw_p, sh_p, hg_s, rw_s, sh_s)
```

```python
import functools

import jax
import jax.numpy as jnp
from jax import lax
from jax.experimental import pallas as pl
from jax.experimental.pallas import tpu as pltpu

F32 = jnp.float32
BF16 = jnp.bfloat16

D_MODEL = 1024
HG_HEADS = 4
HG_DK = 128
HG_DV = 128
HG_WIDTH = HG_HEADS * HG_DV
RW_HEADS = 8
RW_HEAD = 64
RW_WIDTH = RW_HEADS * RW_HEAD
HG_PROJ = 2 * HG_HEADS * HG_DK + 2 * HG_WIDTH
RW_PROJ = 3 * RW_WIDTH
IN_WIDTH = HG_PROJ + RW_PROJ
N_GROUPS = 4
EXPERTS_PER_GROUP = 8
N_EXPERTS = N_GROUPS * EXPERTS_PER_GROUP
TOP_K = 2
D_EXPERT = 512
NORM_EPS = 1e-6
RW_GN_EPS = 64e-5

CHUNK = 64
SUB = 16
ROUTE_W = 128
RW_PACK = 7 * RW_WIDTH
VMEM_LIMIT = 56 * 1024 * 1024

NN = (((1,), (0,)), ((), ()))
NT = (((1,), (1,)), ((), ()))
TN = (((0,), (0,)), ((), ()))


def _split2(a):
    hi = a.astype(BF16)
    lo = (a - hi.astype(F32)).astype(BF16)
    return hi, lo


def _mm(a, b, dims=NN, passes=1):
    if passes == 1:
        return lax.dot_general(a.astype(BF16), b.astype(BF16), dims, preferred_element_type=F32)
    a1, a2 = _split2(a)
    b1, b2 = _split2(b)
    out = lax.dot_general(a1, b1, dims, preferred_element_type=F32)
    out = out + lax.dot_general(a1, b2, dims, preferred_element_type=F32)
    return out + lax.dot_general(a2, b1, dims, preferred_element_type=F32)


def _mm_exact_lhs(a_bf, b):
    b1 = b.astype(BF16)
    r = b - b1.astype(F32)
    b2 = r.astype(BF16)
    b3 = (r - b2.astype(F32)).astype(BF16)
    out = jnp.dot(a_bf, b1, preferred_element_type=F32)
    out = out + jnp.dot(a_bf, b2, preferred_element_type=F32)
    return out + jnp.dot(a_bf, b3, preferred_element_type=F32)


def _sigmoid(x):
    return 1.0 / (1.0 + jnp.exp(-x))


def _silu(x):
    return x * _sigmoid(x)


def _softplus(x):
    return jnp.maximum(x, 0.0) + jnp.log1p(jnp.exp(-jnp.abs(x)))


def _tri_incl_bf16(n):
    r = lax.broadcasted_iota(jnp.int32, (n, n), 0)
    c = lax.broadcasted_iota(jnp.int32, (n, n), 1)
    return (r >= c).astype(BF16)


def _params(sem):
    return pltpu.CompilerParams(dimension_semantics=sem, vmem_limit_bytes=VMEM_LIMIT)


def _ada_kernel(c_ref, w_ref, b_ref, o_ref):
    cs = _silu(c_ref[...])
    o_ref[...] = _mm(cs, w_ref[...]) + b_ref[...]


def _ada_mod(c_all, w_ada, b_ada):
    depth, d, n = w_ada.shape
    rows = c_all.shape[0]
    tn = 1536
    return pl.pallas_call(
        _ada_kernel,
        grid=(depth, n // tn),
        in_specs=[
            pl.BlockSpec((rows, d), lambda l, j: (0, 0)),
            pl.BlockSpec((None, d, tn), lambda l, j: (l, 0, j)),
            pl.BlockSpec((None, 1, tn), lambda l, j: (l, 0, j)),
        ],
        out_specs=pl.BlockSpec((None, rows, tn), lambda l, j: (l, 0, j)),
        out_shape=jax.ShapeDtypeStruct((depth, rows, n), F32),
        compiler_params=_params(("parallel", "parallel")),
        name="ada_mod",
    )(c_all, w_ada, b_ada.reshape(depth, 1, n))


def _premix_kernel(has_vres, *refs):
    if has_vres:
        (x_ref, bvec_ref, vecd_ref, vec5_ref, mup_ref, win_ref, w1_ref, a1_ref, g1_ref, v1_ref,
         w2_ref, a2_ref, g2_ref, v2_ref, seg_ref, vfirst_ref,
         hg_ref, rw_ref, hlast_ref, hcar, pcar) = refs
    else:
        (x_ref, bvec_ref, vecd_ref, vec5_ref, mup_ref, win_ref, w1_ref, a1_ref, g1_ref,
         w2_ref, a2_ref, g2_ref, seg_ref,
         hg_ref, rw_ref, hlast_ref, hcar, pcar) = refs
    t = pl.program_id(1)
    tm = x_ref.shape[0]
    w_rw = win_ref[:, HG_PROJ:]

    @pl.when(t == 0)
    def _():
        hp = jnp.broadcast_to(bvec_ref[2:3, :], (8, D_MODEL))
        hcar[...] = hp
        pcar[...] = _mm(hp, w_rw)

    x = x_ref[...]
    nw = vecd_ref[0:1, :]
    sh = bvec_ref[0:1, :]
    sc = bvec_ref[1:2, :]
    xn = x * lax.rsqrt(jnp.mean(x * x, axis=-1, keepdims=True) + NORM_EPS) * nw
    h = xn * (1.0 + sc) + sh

    hb = h.astype(BF16)
    hg_ref[...] = jnp.dot(hb, win_ref[:, :HG_PROJ], preferred_element_type=F32)
    p_cur = jnp.dot(hb, w_rw, preferred_element_type=F32)

    first = lax.broadcasted_iota(jnp.int32, (tm, 1), 0) == 0
    h_shift = jnp.where(first, hcar[0:1, :], pltpu.roll(h, 1, 0))
    p_shift = jnp.where(first, pcar[0:1, :], pltpu.roll(p_cur, 1, 0))
    hcar[0:1, :] = h[tm - 1:tm, :]
    pcar[0:1, :] = p_cur[tm - 1:tm, :]
    hlast_ref[...] = h[tm - 1:tm, :]

    pm = p_cur + (p_shift - p_cur) * mup_ref[...]
    r = pm[:, :RW_WIDTH]
    k = pm[:, RW_WIDTH:2 * RW_WIDTH]
    v = pm[:, 2 * RW_WIDTH:]

    dh = h_shift - h
    xw = h + dh * vecd_ref[1:2, :]
    xa = h + dh * vecd_ref[2:3, :]
    xg = h + dh * vecd_ref[3:4, :]
    w_pre = vec5_ref[0:1, :] + _mm(jnp.tanh(_mm(xw, w1_ref[...])), w2_ref[...])
    logw = -jnp.exp(-_softplus(-w_pre) - 0.5)
    a = _sigmoid(vec5_ref[1:2, :] + _mm(_mm(xa, a1_ref[...]), a2_ref[...]))
    g = _mm(_sigmoid(_mm(xg, g1_ref[...])), g2_ref[...])
    if has_vres:
        xv = h + dh * vecd_ref[4:5, :]
        nu = _sigmoid(vec5_ref[2:3, :] + _mm(_mm(xv, v1_ref[...]), v2_ref[...]))
        v = v + (vfirst_ref[...] - v) * nu

    kk = k * vec5_ref[3:4, :]
    ss = _mm(kk * kk, seg_ref[...], passes=3)
    kappa = kk / jnp.maximum(jnp.sqrt(ss), 1e-12)
    k_mod = k * (1.0 + (a - 1.0) * vec5_ref[4:5, :])

    rw_ref[:, 0 * RW_WIDTH:1 * RW_WIDTH] = r
    rw_ref[:, 1 * RW_WIDTH:2 * RW_WIDTH] = logw
    rw_ref[:, 2 * RW_WIDTH:3 * RW_WIDTH] = k_mod
    rw_ref[:, 3 * RW_WIDTH:4 * RW_WIDTH] = v
    rw_ref[:, 4 * RW_WIDTH:5 * RW_WIDTH] = kappa
    rw_ref[:, 5 * RW_WIDTH:6 * RW_WIDTH] = kappa * a
    rw_ref[:, 6 * RW_WIDTH:7 * RW_WIDTH] = g


def _premix(x, bvec, vecd, vec5, mup, win, loras, seg, vfirst, tm):
    b, l, d = x.shape
    has_vres = vfirst is not None
    const = lambda shape: pl.BlockSpec(shape, lambda i, j: tuple(0 for _ in shape))
    in_specs = [
        pl.BlockSpec((None, tm, d), lambda i, j: (i, j, 0)),
        pl.BlockSpec((None, 8, d), lambda i, j: (i, 0, 0)),
        const(vecd.shape), const(vec5.shape), const(mup.shape), const(win.shape),
    ]
    in_specs += [const(w.shape) for w in loras]
    in_specs.append(const(seg.shape))
    args = [x, bvec, vecd, vec5, mup, win, *loras, seg]
    if has_vres:
        in_specs.append(pl.BlockSpec((None, tm, RW_WIDTH), lambda i, j: (i, j, 3)))
        args.append(vfirst)
    return pl.pallas_call(
        functools.partial(_premix_kernel, has_vres),
        grid=(b, l // tm),
        in_specs=in_specs,
        out_specs=[
            pl.BlockSpec((None, tm, HG_PROJ), lambda i, j: (i, j, 0)),
            pl.BlockSpec((None, tm, RW_PACK), lambda i, j: (i, j, 0)),
            pl.BlockSpec((None, 1, d), lambda i, j: (i, 0, 0)),
        ],
        out_shape=[
            jax.ShapeDtypeStruct((b, l, HG_PROJ), F32),
            jax.ShapeDtypeStruct((b, l, RW_PACK), F32),
            jax.ShapeDtypeStruct((b, 1, d), F32),
        ],
        scratch_shapes=[pltpu.VMEM((8, d), F32), pltpu.VMEM((8, RW_PROJ), F32)],
        compiler_params=_params(("parallel", "arbitrary")),
        name="premix",
    )(*args)


def _hgrn_chunk(zq, zf, zi, zg, st, loglb, log1mlb, nw, tri):
    c = zq.shape[0]
    q = _silu(zq)
    bterm = log1mlb + (jnp.minimum(zf, 0.0) - jnp.log1p(jnp.exp(-jnp.abs(zf))))
    logf = jnp.maximum(loglb, bterm) + jnp.log1p(jnp.exp(-jnp.abs(loglb - bterm)))
    k = 1.0 - jnp.exp(logf)
    v = zi
    bcum = _mm_exact_lhs(tri, logf)
    rows = lax.broadcasted_iota(jnp.int32, (SUB, 1), 0)
    outs = []
    for i in range(c // SUB):
        lo = i * SUB
        bi = bcum[lo:lo + SUB]
        qi = q[lo:lo + SUB]
        ki = k[lo:lo + SUB]
        vi = v[lo:lo + SUB]
        o = _mm(qi * jnp.exp(bi), st, NT)
        if i > 0:
            b0 = bcum[lo - 1:lo]
            qt = qi * jnp.exp(bi - b0)
            kt = k[:lo] * jnp.exp(b0 - bcum[:lo])
            o = o + _mm(_mm(qt, kt, NT), v[:lo])
        for s in range(SUB):
            dec = jnp.exp(jnp.minimum(bi - bi[s:s + 1], 0.0))
            col = jnp.sum(qi * ki[s:s + 1] * dec, axis=-1, keepdims=True)
            o = o + jnp.where(rows >= s, col, 0.0) * vi[s:s + 1]
        outs.append(o)
    o = outs[0] if len(outs) == 1 else jnp.concatenate(outs, axis=0)
    b_last = bcum[c - 1:c]
    st_new = st * jnp.exp(b_last) + _mm(v, k * jnp.exp(b_last - bcum), TN)
    on = o * lax.rsqrt(jnp.mean(o * o, axis=-1, keepdims=True) + NORM_EPS) * nw
    return on * _silu(zg), st_new


def _hgrn_kernel(c, hg_ref, lb_ref, nw_ref, s0_ref, o_ref, sfin_ref):
    t = pl.program_id(1)

    @pl.when(t == 0)
    def _():
        sfin_ref[...] = s0_ref[...]

    tri = _tri_incl_bf16(c)
    qk = HG_HEADS * HG_DK
    for ci in range(hg_ref.shape[0] // c):
        r0 = ci * c
        for h in range(HG_HEADS):
            zq = hg_ref[r0:r0 + c, h * HG_DK:(h + 1) * HG_DK]
            zf = hg_ref[r0:r0 + c, qk + h * HG_DK:qk + (h + 1) * HG_DK]
            zi = hg_ref[r0:r0 + c, 2 * qk + h * HG_DV:2 * qk + (h + 1) * HG_DV]
            zg = hg_ref[r0:r0 + c, 2 * qk + HG_WIDTH + h * HG_DV:2 * qk + HG_WIDTH + (h + 1) * HG_DV]
            sl = slice(h * HG_DK, (h + 1) * HG_DK)
            out, st_new = _hgrn_chunk(zq, zf, zi, zg, sfin_ref[h], lb_ref[0:1, sl], lb_ref[1:2, sl],
                                      nw_ref[0:1, h * HG_DV:(h + 1) * HG_DV], tri)
            sfin_ref[h] = st_new
            o_ref[r0:r0 + c, h * HG_DV:(h + 1) * HG_DV] = out


def _hgrn(hgraw, lbvec, nw, s0t, ch, c):
    b, l, _ = hgraw.shape
    return pl.pallas_call(
        functools.partial(_hgrn_kernel, c),
        grid=(b, l // ch),
        in_specs=[
            pl.BlockSpec((None, ch, HG_PROJ), lambda i, j: (i, j, 0)),
            pl.BlockSpec(lbvec.shape, lambda i, j: (0, 0)),
            pl.BlockSpec(nw.shape, lambda i, j: (0, 0)),
            pl.BlockSpec((None, HG_HEADS, HG_DV, HG_DK), lambda i, j: (i, 0, 0, 0)),
        ],
        out_specs=[
            pl.BlockSpec((None, ch, HG_WIDTH), lambda i, j: (i, j, 0)),
            pl.BlockSpec((None, HG_HEADS, HG_DV, HG_DK), lambda i, j: (i, 0, 0, 0)),
        ],
        out_shape=[
            jax.ShapeDtypeStruct((b, l, HG_WIDTH), F32),
            jax.ShapeDtypeStruct((b, HG_HEADS, HG_DV, HG_DK), F32),
        ],
        compiler_params=_params(("parallel", "arbitrary")),
        name="hgrn",
    )(hgraw, lbvec, nw, s0t)


RW_PASSES_A = 1
RW_PASSES_B = 1


def _rwkv_chunk_matrices(c, r, lw, km, v, kap, beta, tri, strict, incl, same_sub):
    pa = RW_PASSES_A
    hs = range(RW_HEADS)
    sl = [slice(h * RW_HEAD, (h + 1) * RW_HEAD) for h in hs]
    cum = _mm_exact_lhs(tri, lw)
    c_last = cum[c - 1:c]
    e_neg = jnp.exp(-cum)
    e_hat = jnp.exp(c_last - cum)
    kbar = kap * jnp.exp(cum - lw)
    rbar = r * jnp.exp(cum)
    bneg = beta * e_neg
    kneg = km * e_neg
    bhat = beta * e_hat
    khat = km * e_hat
    e_last = jnp.exp(c_last)
    mask_kq = jnp.concatenate([strict, incl], axis=0)
    lhs = [jnp.concatenate([kbar[:, sl[h]], rbar[:, sl[h]]], axis=0) for h in hs]
    rhs = [jnp.concatenate([bneg[:, sl[h]], kneg[:, sl[h]]], axis=0) for h in hs]
    gmat = [_mm(lhs[h], rhs[h], NT, passes=pa) for h in hs]
    aab = [jnp.where(strict, gmat[h][:c, :c], 0.0) for h in hs]
    aqb = [jnp.where(incl, gmat[h][c:, :c], 0.0) for h in hs]
    akq = [jnp.where(mask_kq, gmat[h][:, c:], 0.0) for h in hs]
    av = [_mm(akq[h], v[:, sl[h]], passes=pa) for h in hs]
    d1 = [jnp.where(same_sub, aab[h], 0.0) for h in hs]
    if c == SUB:
        z = [jnp.concatenate([kbar[:, sl[h]], av[h][:c]], axis=1) for h in hs]
    else:
        z = [jnp.concatenate([aab[h] - d1[h], kbar[:, sl[h]], av[h][:c]], axis=1) for h in hs]
    d2 = [_mm(d1[h], d1[h], passes=pa) for h in hs]
    z = [z[h] - _mm(d1[h], z[h], passes=pa) for h in hs]
    d4 = [_mm(d2[h], d2[h], passes=pa) for h in hs]
    z = [z[h] + _mm(d2[h], z[h], passes=pa) for h in hs]
    d8 = [_mm(d4[h], d4[h], passes=pa) for h in hs]
    z = [z[h] + _mm(d4[h], z[h], passes=pa) for h in hs]
    z = [z[h] + _mm(d8[h], z[h], passes=pa) for h in hs]
    if c == SUB:
        x = z
    else:
        f = [z[h][:, :c] for h in hs]
        x = [z[h][:, c:] for h in hs]
        fx = [_mm(f[h], x[h], passes=pa) for h in hs]
        x = [x[h] + _mm(f[h], fx[h], passes=pa) for h in hs]
        x = [x[h] - _mm(f[h], x[h], passes=pa) for h in hs]
    qb = [_mm(aqb[h], x[h], passes=pa) for h in hs]
    rt = [rbar[:, sl[h]] - qb[h][:, :RW_HEAD] for h in hs]
    yc = [av[h][c:] - qb[h][:, RW_HEAD:] for h in hs]
    ktb = [_mm(x[h][:, :RW_HEAD], bhat[:, sl[h]], TN, passes=pa) for h in hs]
    nc = [_mm(jnp.concatenate([v[:, sl[h]], x[h][:, RW_HEAD:]], axis=0),
              jnp.concatenate([khat[:, sl[h]], -bhat[:, sl[h]]], axis=0), TN, passes=pa) for h in hs]
    return rt, yc, ktb, nc, e_last


def _rwkv_kernel(c, rw_ref, vec_ref, s0_ref, o_ref, sfin_ref):
    t = pl.program_id(1)

    @pl.when(t == 0)
    def _():
        sfin_ref[...] = s0_ref[...]

    tri = _tri_incl_bf16(c)
    row = lax.broadcasted_iota(jnp.int32, (c, c), 0)
    col = lax.broadcasted_iota(jnp.int32, (c, c), 1)
    strict = row > col
    incl = row >= col
    same_sub = (row // SUB) == (col // SUB)
    w = RW_WIDTH
    hs = range(RW_HEADS)
    sl = [slice(h * RW_HEAD, (h + 1) * RW_HEAD) for h in hs]
    state = [sfin_ref[h] for h in hs]
    for ci in range(rw_ref.shape[0] // c):
        r0 = ci * c
        r = rw_ref[r0:r0 + c, 0 * w:1 * w]
        lw = rw_ref[r0:r0 + c, 1 * w:2 * w]
        km = rw_ref[r0:r0 + c, 2 * w:3 * w]
        v = rw_ref[r0:r0 + c, 3 * w:4 * w]
        kap = rw_ref[r0:r0 + c, 4 * w:5 * w]
        beta = rw_ref[r0:r0 + c, 5 * w:6 * w]
        gate = rw_ref[r0:r0 + c, 6 * w:7 * w]
        rt, yc, ktb, nc, e_last = _rwkv_chunk_matrices(c, r, lw, km, v, kap, beta, tri, strict, incl, same_sub)
        y = [_mm(rt[h], state[h], NT, passes=RW_PASSES_B) + yc[h] for h in hs]
        state = [state[h] * e_last[:, sl[h]] - _mm(state[h], ktb[h], passes=RW_PASSES_B) + nc[h] for h in hs]
        bonus = r * km * vec_ref[2:3, :]
        for h in hs:
            mean = jnp.mean(y[h], axis=-1, keepdims=True)
            yd = y[h] - mean
            var = jnp.mean(yd * yd, axis=-1, keepdims=True)
            yn = yd * lax.rsqrt(var + RW_GN_EPS) * vec_ref[0:1, sl[h]] + vec_ref[1:2, sl[h]]
            yn = yn + jnp.sum(bonus[:, sl[h]], axis=-1, keepdims=True) * v[:, sl[h]]
            o_ref[r0:r0 + c, sl[h]] = yn * gate[:, sl[h]]
    for h in hs:
        sfin_ref[h] = state[h]


def _rwkv(rwp, vec, s0, ch, c):
    b, l, _ = rwp.shape
    return pl.pallas_call(
        functools.partial(_rwkv_kernel, c),
        grid=(b, l // ch),
        in_specs=[
            pl.BlockSpec((None, ch, RW_PACK), lambda i, j: (i, j, 0)),
            pl.BlockSpec(vec.shape, lambda i, j: (0, 0)),
            pl.BlockSpec((None, RW_HEADS, RW_HEAD, RW_HEAD), lambda i, j: (i, 0, 0, 0)),
        ],
        out_specs=[
            pl.BlockSpec((None, ch, RW_WIDTH), lambda i, j: (i, j, 0)),
            pl.BlockSpec((None, RW_HEADS, RW_HEAD, RW_HEAD), lambda i, j: (i, 0, 0, 0)),
        ],
        out_shape=[
            jax.ShapeDtypeStruct((b, l, RW_WIDTH), F32),
            jax.ShapeDtypeStruct((b, RW_HEADS, RW_HEAD, RW_HEAD), F32),
        ],
        compiler_params=_params(("parallel", "arbitrary")),
        name="rwkv",
    )(rwp, vec, s0)


def _postmix_kernel(ohg_ref, yrw_ref, x_ref, bvec_ref, nw_ref, wout_ref, wr_ref, br_ref,
                    xn_ref, h2_ref, route_ref):
    m = (jnp.dot(ohg_ref[...].astype(BF16), wout_ref[:HG_WIDTH, :], preferred_element_type=F32)
         + jnp.dot(yrw_ref[...].astype(BF16), wout_ref[HG_WIDTH:, :], preferred_element_type=F32))
    xn = x_ref[...] + bvec_ref[0:1, :] * m
    xn_ref[...] = xn
    h2 = (xn * lax.rsqrt(jnp.mean(xn * xn, axis=-1, keepdims=True) + NORM_EPS) * nw_ref[...]
          * (1.0 + bvec_ref[2:3, :]) + bvec_ref[1:2, :])
    h2_ref[...] = h2

    logits = _mm(h2, wr_ref[...]) + br_ref[...]
    lane = lax.broadcasted_iota(jnp.int32, logits.shape, 1)
    neg = jnp.float32(-jnp.inf)
    big = jnp.int32(ROUTE_W)
    is_g = lane < N_GROUPS
    gl = jnp.where(is_g, logits, neg)
    gmax = jnp.max(gl, axis=-1, keepdims=True)
    grp = jnp.min(jnp.where(gl == gmax, lane, big), axis=-1, keepdims=True)
    p_grp = 1.0 / jnp.sum(jnp.where(is_g, jnp.exp(gl - gmax), 0.0), axis=-1, keepdims=True)
    e_lo = N_GROUPS + grp * EXPERTS_PER_GROUP
    in_grp = (lane >= e_lo) & (lane < e_lo + EXPERTS_PER_GROUP)
    el = jnp.where(in_grp, logits, neg)
    m1 = jnp.max(el, axis=-1, keepdims=True)
    i1 = jnp.min(jnp.where(el == m1, lane, big), axis=-1, keepdims=True)
    el2 = jnp.where(lane == i1, neg, el)
    m2 = jnp.max(el2, axis=-1, keepdims=True)
    i2 = jnp.min(jnp.where(el2 == m2, lane, big), axis=-1, keepdims=True)
    e21 = jnp.exp(m2 - m1)
    p1 = 1.0 / (1.0 + e21)
    p2 = e21 / (1.0 + e21)
    out = jnp.where(lane == 0, (i1 - N_GROUPS).astype(F32),
                    jnp.where(lane == 1, (i2 - N_GROUPS).astype(F32),
                              jnp.where(lane == 2, p_grp * p1,
                                        jnp.where(lane == 3, p_grp * p2, 0.0))))
    route_ref[...] = out[:, :8]


def _postmix(ohg, yrw, x, bvec, nw, wout, wr, br, tm):
    b, l, d = x.shape
    const = lambda shape: pl.BlockSpec(shape, lambda i, j: tuple(0 for _ in shape))
    return pl.pallas_call(
        _postmix_kernel,
        grid=(b, l // tm),
        in_specs=[
            pl.BlockSpec((None, tm, HG_WIDTH), lambda i, j: (i, j, 0)),
            pl.BlockSpec((None, tm, RW_WIDTH), lambda i, j: (i, j, 0)),
            pl.BlockSpec((None, tm, d), lambda i, j: (i, j, 0)),
            pl.BlockSpec((None, 8, d), lambda i, j: (i, 0, 0)),
            const(nw.shape), const(wout.shape), const(wr.shape), const(br.shape),
        ],
        out_specs=[
            pl.BlockSpec((None, tm, d), lambda i, j: (i, j, 0)),
            pl.BlockSpec((None, tm, d), lambda i, j: (i, j, 0)),
            pl.BlockSpec((None, tm, 8), lambda i, j: (i, j, 0)),
        ],
        out_shape=[
            jax.ShapeDtypeStruct((b, l, d), F32),
            jax.ShapeDtypeStruct((b, l, d), F32),
            jax.ShapeDtypeStruct((b, l, 8), F32),
        ],
        compiler_params=_params(("parallel", "parallel")),
        name="postmix",
    )(ohg, yrw, x, bvec, nw, wout, wr, br)


def _expert_kernel(be_ref, nu_ref, xs_ref, wg_ref, wu_ref, wd_ref, o_ref):
    i = pl.program_id(0)

    @pl.when(i < nu_ref[0])
    def _():
        xb = xs_ref[...].astype(BF16)
        hg = jnp.dot(xb, wg_ref[...], preferred_element_type=F32)
        hu = jnp.dot(xb, wu_ref[...], preferred_element_type=F32)
        hdn = (_silu(hg) * hu).astype(BF16)
        o_ref[...] = jnp.dot(hdn, wd_ref[...], preferred_element_type=F32)

    @pl.when(i >= nu_ref[0])
    def _():
        o_ref[...] = jnp.zeros_like(o_ref)


def _experts(xs, block_e, n_used, wg, wu, wd, blk):
    p, d = xs.shape
    nb = p // blk
    grid_spec = pltpu.PrefetchScalarGridSpec(
        num_scalar_prefetch=2,
        grid=(nb,),
        in_specs=[
            pl.BlockSpec((blk, d), lambda i, be, nu: (i, 0)),
            pl.BlockSpec((None, d, D_EXPERT), lambda i, be, nu: (be[i], 0, 0)),
            pl.BlockSpec((None, d, D_EXPERT), lambda i, be, nu: (be[i], 0, 0)),
            pl.BlockSpec((None, D_EXPERT, d), lambda i, be, nu: (be[i], 0, 0)),
        ],
        out_specs=pl.BlockSpec((blk, d), lambda i, be, nu: (i, 0)),
    )
    return pl.pallas_call(
        _expert_kernel,
        grid_spec=grid_spec,
        out_shape=jax.ShapeDtypeStruct((p, d), F32),
        compiler_params=_params(("arbitrary",)),
        name="experts",
    )(block_e, n_used, xs, wg, wu, wd)


def _combine_kernel(final, xn_ref, y0_ref, y1_ref, route_ref, bvec_ref, fw_ref, o_ref):
    rt = route_ref[...]
    moe = rt[:, 2:3] * y0_ref[...] + rt[:, 3:4] * y1_ref[...]
    x = xn_ref[...] + bvec_ref[3:4, :] * moe
    if final:
        x = x * lax.rsqrt(jnp.mean(x * x, axis=-1, keepdims=True) + NORM_EPS) * fw_ref[...]
    o_ref[...] = x


def _combine(xn, y0, y1, route, bvec, fw, final, tm):
    b, l, d = xn.shape
    row = lambda w: pl.BlockSpec((None, tm, w), lambda i, j: (i, j, 0))
    return pl.pallas_call(
        functools.partial(_combine_kernel, final),
        grid=(b, l // tm),
        in_specs=[row(d), row(d), row(d), row(8),
                  pl.BlockSpec((None, 8, d), lambda i, j: (i, 0, 0)),
                  pl.BlockSpec((1, d), lambda i, j: (0, 0))],
        out_specs=row(d),
        out_shape=jax.ShapeDtypeStruct((b, l, d), F32),
        compiler_params=_params(("parallel", "parallel")),
        name="combine",
    )(xn, y0, y1, route, bvec, fw)


def _dispatch(route, blk):
    t = route.shape[0]
    n = t * TOP_K
    flat_e = route[:, :TOP_K].astype(jnp.int32).reshape(n)
    order = jnp.argsort(flat_e)
    sorted_e = flat_e[order]
    counts = jnp.bincount(flat_e, length=N_EXPERTS)
    padded = ((counts + blk - 1) // blk) * blk
    start = jnp.cumsum(counts) - counts
    pend = jnp.cumsum(padded)
    pstart = pend - padded
    dest = pstart[sorted_e] + jnp.arange(n, dtype=jnp.int32) - start[sorted_e]
    nb = -(-(n + N_EXPERTS * (blk - 1)) // blk)
    buf_tok = jnp.zeros((nb * blk,), jnp.int32).at[dest].set((order // TOP_K).astype(jnp.int32))
    slot = jnp.zeros((n,), jnp.int32).at[order].set(dest.astype(jnp.int32))
    block_e = jnp.minimum(jnp.searchsorted(pend // blk, jnp.arange(nb), side='right'),
                          N_EXPERTS - 1).astype(jnp.int32)
    n_used = (pend[-1] // blk).astype(jnp.int32).reshape(1)
    return buf_tok, slot.reshape(t, TOP_K), block_e, n_used


def _layer_weights(p, l):
    f = lambda name: p[name][l]
    zero_d = jnp.zeros((D_MODEL,), F32)
    has_vres = l > 0
    vecd = jnp.stack([f('norm_mix_w'), p['rw_mu_lora'][l, 0], p['rw_mu_lora'][l, 1], p['rw_mu_lora'][l, 2],
                      p['rw_mu_vres'][l - 1] if has_vres else zero_d, zero_d, zero_d, zero_d])
    zero_w = jnp.zeros((RW_WIDTH,), F32)
    vec5 = jnp.stack([f('rw_w0'), f('rw_a0'), p['rw_v0'][l - 1] if has_vres else zero_w,
                      f('rw_kk'), f('rw_ka'), zero_w, zero_w, zero_w])
    loras = [f('rw_w1'), f('rw_a1'), f('rw_g1')] + ([p['rw_v1'][l - 1]] if has_vres else [])
    loras += [f('rw_w2'), f('rw_a2'), f('rw_g2')] + ([p['rw_v2'][l - 1]] if has_vres else [])
    loras = [w.astype(BF16) for w in loras]
    rwvec = jnp.stack([f('rw_ln_w'), f('rw_ln_b'), f('rw_rk').reshape(RW_WIDTH),
                       zero_w, zero_w, zero_w, zero_w, zero_w])
    wr = jnp.concatenate([f('router_g_w'),
                          jnp.transpose(f('router_e_w'), (1, 0, 2)).reshape(D_MODEL, N_EXPERTS),
                          jnp.zeros((D_MODEL, ROUTE_W - N_GROUPS - N_EXPERTS), F32)], axis=1)
    br = jnp.concatenate([f('router_g_b'), f('router_e_b').reshape(N_EXPERTS),
                          jnp.zeros((ROUTE_W - N_GROUPS - N_EXPERTS,), F32)]).reshape(1, ROUTE_W)
    return dict(
        vecd=vecd, vec5=vec5, mup=f('rw_mu_proj').reshape(1, RW_PROJ), win=f('w_in').astype(BF16),
        loras=loras, rwvec=rwvec, hg_nw=f('hg_norm_w').reshape(1, HG_WIDTH),
        nfw=f('norm_ffn_w').reshape(1, D_MODEL), wout=f('w_out').astype(BF16), wr=wr, br=br,
        wg=f('w_gate').astype(BF16), wu=f('w_up').astype(BF16), wd=f('w_down').astype(BF16))


def _trunk(x, mod, s_hg, s_rw, h_prev, lw_all, lbvecs, final_w):
    b, l, d = x.shape
    depth = len(lw_all)
    tm = min(l, 256)
    ch = min(l, 256)
    c = min(l, CHUNK)
    blk = 256 if b * l * TOP_K >= 8192 else 32
    head = lax.broadcasted_iota(jnp.int32, (RW_WIDTH, RW_WIDTH), 0) // RW_HEAD
    seg = (head == head.T).astype(BF16)
    new_hg, new_rw, new_shift = [], [], []
    rwp_first = None
    zrow = jnp.zeros((b, d), F32)
    for li in range(depth):
        w = lw_all[li]
        sh1, sc1, g1, sh2, sc2, g2 = jnp.split(mod[li], 6, axis=-1)
        bvec1 = jnp.stack([sh1, sc1, h_prev[li], zrow, zrow, zrow, zrow, zrow], axis=1)
        bvec2 = jnp.stack([g1, sh2, sc2, g2, zrow, zrow, zrow, zrow], axis=1)
        hgraw, rwp, hlast = _premix(x, bvec1, w['vecd'], w['vec5'], w['mup'], w['win'], w['loras'],
                                    seg, rwp_first if li > 0 else None, tm)
        if li == 0:
            rwp_first = rwp
        ohg, hg_t = _hgrn(hgraw, lbvecs[li], w['hg_nw'], jnp.swapaxes(s_hg[li], -1, -2), ch, c)
        yrw, rw_fin = _rwkv(rwp, w['rwvec'], s_rw[li], ch, c)
        xn, h2, route = _postmix(ohg, yrw, x, bvec2, w['nfw'], w['wout'], w['wr'], w['br'], tm)
        t = b * l
        buf_tok, slot, block_e, n_used = _dispatch(route.reshape(t, 8), blk)
        xs = jnp.take(h2.reshape(t, d), buf_tok, axis=0)
        yb = _experts(xs, block_e, n_used, w['wg'], w['wu'], w['wd'], blk)
        y0 = jnp.take(yb, slot[:, 0], axis=0).reshape(b, l, d)
        y1 = jnp.take(yb, slot[:, 1], axis=0).reshape(b, l, d)
        x = _combine(xn, y0, y1, route, bvec2, final_w, li == depth - 1, tm)
        new_hg.append(jnp.swapaxes(hg_t, -1, -2))
        new_rw.append(rw_fin)
        new_shift.append(hlast.reshape(b, d))
    return x, jnp.stack(new_hg), jnp.stack(new_rw), jnp.stack(new_shift)


def kernel(x_prompt, x_sample, state_hgrn, state_rwkv, state_shift, c_prompt, c_sample, w_ada, b_ada, norm_mix_w, norm_ffn_w, w_in, w_out, hg_lb_logits, hg_norm_w, rw_mu_proj, rw_mu_lora, rw_w0, rw_w1, rw_w2, rw_a0, rw_a1, rw_a2, rw_g1, rw_g2, rw_mu_vres, rw_v0, rw_v1, rw_v2, rw_kk, rw_ka, rw_rk, rw_ln_w, rw_ln_b, router_g_w, router_g_b, router_e_w, router_e_b, w_gate, w_up, w_down, final_norm_w):
    p = dict(norm_mix_w=norm_mix_w, norm_ffn_w=norm_ffn_w, w_in=w_in, w_out=w_out, hg_norm_w=hg_norm_w,
             rw_mu_proj=rw_mu_proj, rw_mu_lora=rw_mu_lora, rw_w0=rw_w0, rw_w1=rw_w1, rw_w2=rw_w2,
             rw_a0=rw_a0, rw_a1=rw_a1, rw_a2=rw_a2, rw_g1=rw_g1, rw_g2=rw_g2,
             rw_mu_vres=rw_mu_vres, rw_v0=rw_v0, rw_v1=rw_v1, rw_v2=rw_v2,
             rw_kk=rw_kk, rw_ka=rw_ka, rw_rk=rw_rk, rw_ln_w=rw_ln_w, rw_ln_b=rw_ln_b,
             router_g_w=router_g_w, router_g_b=router_g_b, router_e_w=router_e_w,
             router_e_b=router_e_b, w_gate=w_gate, w_up=w_up, w_down=w_down)
    depth = w_in.shape[0]
    bp = x_prompt.shape[0]
    bs = x_sample.shape[0]
    d = x_prompt.shape[-1]
    lw_all = [_layer_weights(p, l) for l in range(depth)]
    lb = jnp.cumsum(jax.nn.softmax(hg_lb_logits.astype(F32), axis=0), axis=0)
    lb = lb - lb[0]
    zpad = jnp.zeros((6, lb.shape[1]), F32)
    lbvecs = [jnp.concatenate([jnp.log(lb[l])[None], jnp.log1p(-lb[l])[None], zpad], axis=0)
              for l in range(depth)]
    rows = -(-(bp + bs) // 8) * 8
    c_all = jnp.concatenate([c_prompt, c_sample, jnp.zeros((rows - bp - bs, d), F32)], axis=0)
    mod = _ada_mod(c_all, w_ada, b_ada)
    fw = final_norm_w.reshape(1, d)
    zero_hg = jnp.zeros((depth, bp) + state_hgrn.shape[2:], F32)
    zero_rw = jnp.zeros((depth, bp) + state_rwkv.shape[2:], F32)
    zero_sh = jnp.zeros((depth, bp, d), F32)
    y_p, hg_p, rw_p, sh_p = _trunk(x_prompt, mod[:, :bp], zero_hg, zero_rw, zero_sh, lw_all, lbvecs, fw)
    y_s, hg_s, rw_s, sh_s = _trunk(x_sample, mod[:, bp:bp + bs], state_hgrn, state_rwkv, state_shift,
                                   lw_all, lbvecs, fw)
    return (y_p, y_s, hg_p, rw_p, sh_p, hg_s, rw_s, sh_s)
```

```python
import functools

import jax
import jax.numpy as jnp
from jax import lax
from jax.experimental import pallas as pl
from jax.experimental.pallas import tpu as pltpu

F32 = jnp.float32
BF16 = jnp.bfloat16

D_MODEL = 1024
HG_HEADS = 4
HG_DK = 128
HG_DV = 128
HG_WIDTH = HG_HEADS * HG_DV
RW_HEADS = 8
RW_HEAD = 64
RW_WIDTH = RW_HEADS * RW_HEAD
HG_PROJ = 2 * HG_HEADS * HG_DK + 2 * HG_WIDTH
RW_PROJ = 3 * RW_WIDTH
IN_WIDTH = HG_PROJ + RW_PROJ
N_GROUPS = 4
EXPERTS_PER_GROUP = 8
N_EXPERTS = N_GROUPS * EXPERTS_PER_GROUP
TOP_K = 2
D_EXPERT = 512
NORM_EPS = 1e-6
RW_GN_EPS = 64e-5

CHUNK = 64
SUB = 16
ROUTE_W = 128
RW_PACK = 7 * RW_WIDTH
VMEM_LIMIT = 56 * 1024 * 1024

NN = (((1,), (0,)), ((), ()))
NT = (((1,), (1,)), ((), ()))
TN = (((0,), (0,)), ((), ()))


def _split2(a):
    hi = a.astype(BF16)
    lo = (a - hi.astype(F32)).astype(BF16)
    return hi, lo


def _mm(a, b, dims=NN, passes=1):
    if passes == 1:
        return lax.dot_general(a.astype(BF16), b.astype(BF16), dims, preferred_element_type=F32)
    a1, a2 = _split2(a)
    b1, b2 = _split2(b)
    out = lax.dot_general(a1, b1, dims, preferred_element_type=F32)
    out = out + lax.dot_general(a1, b2, dims, preferred_element_type=F32)
    return out + lax.dot_general(a2, b1, dims, preferred_element_type=F32)


def _mm_exact_lhs(a_bf, b):
    b1 = b.astype(BF16)
    r = b - b1.astype(F32)
    b2 = r.astype(BF16)
    b3 = (r - b2.astype(F32)).astype(BF16)
    out = jnp.dot(a_bf, b1, preferred_element_type=F32)
    out = out + jnp.dot(a_bf, b2, preferred_element_type=F32)
    return out + jnp.dot(a_bf, b3, preferred_element_type=F32)


def _sigmoid(x):
    return 1.0 / (1.0 + jnp.exp(-x))


def _silu(x):
    return x * _sigmoid(x)


def _softplus(x):
    return jnp.maximum(x, 0.0) + jnp.log1p(jnp.exp(-jnp.abs(x)))


def _tri_incl_bf16(n):
    r = lax.broadcasted_iota(jnp.int32, (n, n), 0)
    c = lax.broadcasted_iota(jnp.int32, (n, n), 1)
    return (r >= c).astype(BF16)


def _params(sem):
    return pltpu.CompilerParams(dimension_semantics=sem, vmem_limit_bytes=VMEM_LIMIT)


def _ada_kernel(c_ref, w_ref, b_ref, o_ref):
    cs = _silu(c_ref[...])
    o_ref[...] = _mm(cs, w_ref[...]) + b_ref[...]


def _ada_mod(c_all, w_ada, b_ada):
    depth, d, n = w_ada.shape
    rows = c_all.shape[0]
    tn = 1536
    return pl.pallas_call(
        _ada_kernel,
        grid=(depth, n // tn),
        in_specs=[
            pl.BlockSpec((rows, d), lambda l, j: (0, 0)),
            pl.BlockSpec((None, d, tn), lambda l, j: (l, 0, j)),
            pl.BlockSpec((None, 1, tn), lambda l, j: (l, 0, j)),
        ],
        out_specs=pl.BlockSpec((None, rows, tn), lambda l, j: (l, 0, j)),
        out_shape=jax.ShapeDtypeStruct((depth, rows, n), F32),
        compiler_params=_params(("parallel", "parallel")),
        name="ada_mod",
    )(c_all, w_ada, b_ada.reshape(depth, 1, n))


def _premix_kernel(has_vres, *refs):
    if has_vres:
        (x_ref, bvec_ref, vecd_ref, vec5_ref, mup_ref, win_ref, w1_ref, a1_ref, g1_ref, v1_ref,
         w2_ref, a2_ref, g2_ref, v2_ref, seg_ref, vfirst_ref,
         hg_ref, rw_ref, hlast_ref, hcar, pcar) = refs
    else:
        (x_ref, bvec_ref, vecd_ref, vec5_ref, mup_ref, win_ref, w1_ref, a1_ref, g1_ref,
         w2_ref, a2_ref, g2_ref, seg_ref,
         hg_ref, rw_ref, hlast_ref, hcar, pcar) = refs
    t = pl.program_id(1)
    tm = x_ref.shape[0]
    w_rw = win_ref[:, HG_PROJ:]

    @pl.when(t == 0)
    def _():
        hp = jnp.broadcast_to(bvec_ref[2:3, :], (8, D_MODEL))
        hcar[...] = hp
        pcar[...] = _mm(hp, w_rw)

    x = x_ref[...]
    nw = vecd_ref[0:1, :]
    sh = bvec_ref[0:1, :]
    sc = bvec_ref[1:2, :]
    xn = x * lax.rsqrt(jnp.mean(x * x, axis=-1, keepdims=True) + NORM_EPS) * nw
    h = xn * (1.0 + sc) + sh

    hb = h.astype(BF16)
    hg_ref[...] = jnp.dot(hb, win_ref[:, :HG_PROJ], preferred_element_type=F32)
    p_cur = jnp.dot(hb, w_rw, preferred_element_type=F32)

    first = lax.broadcasted_iota(jnp.int32, (tm, 1), 0) == 0
    h_shift = jnp.where(first, hcar[0:1, :], pltpu.roll(h, 1, 0))
    p_shift = jnp.where(first, pcar[0:1, :], pltpu.roll(p_cur, 1, 0))
    hcar[0:1, :] = h[tm - 1:tm, :]
    pcar[0:1, :] = p_cur[tm - 1:tm, :]
    hlast_ref[...] = h[tm - 1:tm, :]

    pm = p_cur + (p_shift - p_cur) * mup_ref[...]
    r = pm[:, :RW_WIDTH]
    k = pm[:, RW_WIDTH:2 * RW_WIDTH]
    v = pm[:, 2 * RW_WIDTH:]

    dh = h_shift - h
    xw = h + dh * vecd_ref[1:2, :]
    xa = h + dh * vecd_ref[2:3, :]
    xg = h + dh * vecd_ref[3:4, :]
    w_pre = vec5_ref[0:1, :] + _mm(jnp.tanh(_mm(xw, w1_ref[...])), w2_ref[...])
    logw = -jnp.exp(-_softplus(-w_pre) - 0.5)
    a = _sigmoid(vec5_ref[1:2, :] + _mm(_mm(xa, a1_ref[...]), a2_ref[...]))
    g = _mm(_sigmoid(_mm(xg, g1_ref[...])), g2_ref[...])
    if has_vres:
        xv = h + dh * vecd_ref[4:5, :]
        nu = _sigmoid(vec5_ref[2:3, :] + _mm(_mm(xv, v1_ref[...]), v2_ref[...]))
        v = v + (vfirst_ref[...] - v) * nu

    kk = k * vec5_ref[3:4, :]
    ss = _mm(kk * kk, seg_ref[...], passes=3)
    kappa = kk / jnp.maximum(jnp.sqrt(ss), 1e-12)
    k_mod = k * (1.0 + (a - 1.0) * vec5_ref[4:5, :])

    rw_ref[:, 0 * RW_WIDTH:1 * RW_WIDTH] = r
    rw_ref[:, 1 * RW_WIDTH:2 * RW_WIDTH] = logw
    rw_ref[:, 2 * RW_WIDTH:3 * RW_WIDTH] = k_mod
    rw_ref[:, 3 * RW_WIDTH:4 * RW_WIDTH] = v
    rw_ref[:, 4 * RW_WIDTH:5 * RW_WIDTH] = kappa
    rw_ref[:, 5 * RW_WIDTH:6 * RW_WIDTH] = kappa * a
    rw_ref[:, 6 * RW_WIDTH:7 * RW_WIDTH] = g


def _premix(x, bvec, vecd, vec5, mup, win, loras, seg, vfirst, tm):
    b, l, d = x.shape
    has_vres = vfirst is not None
    const = lambda shape: pl.BlockSpec(shape, lambda i, j: tuple(0 for _ in shape))
    in_specs = [
        pl.BlockSpec((None, tm, d), lambda i, j: (i, j, 0)),
        pl.BlockSpec((None, 8, d), lambda i, j: (i, 0, 0)),
        const(vecd.shape), const(vec5.shape), const(mup.shape), const(win.shape),
    ]
    in_specs += [const(w.shape) for w in loras]
    in_specs.append(const(seg.shape))
    args = [x, bvec, vecd, vec5, mup, win, *loras, seg]
    if has_vres:
        in_specs.append(pl.BlockSpec((None, tm, RW_WIDTH), lambda i, j: (i, j, 3)))
        args.append(vfirst)
    return pl.pallas_call(
        functools.partial(_premix_kernel, has_vres),
        grid=(b, l // tm),
        in_specs=in_specs,
        out_specs=[
            pl.BlockSpec((None, tm, HG_PROJ), lambda i, j: (i, j, 0)),
            pl.BlockSpec((None, tm, RW_PACK), lambda i, j: (i, j, 0)),
            pl.BlockSpec((None, 1, d), lambda i, j: (i, 0, 0)),
        ],
        out_shape=[
            jax.ShapeDtypeStruct((b, l, HG_PROJ), F32),
            jax.ShapeDtypeStruct((b, l, RW_PACK), F32),
            jax.ShapeDtypeStruct((b, 1, d), F32),
        ],
        scratch_shapes=[pltpu.VMEM((8, d), F32), pltpu.VMEM((8, RW_PROJ), F32)],
        compiler_params=_params(("parallel", "arbitrary")),
        name="premix",
    )(*args)


def _hgrn_chunk(zq, zf, zi, zg, st, loglb, log1mlb, nw, tri):
    c = zq.shape[0]
    q = _silu(zq)
    bterm = log1mlb + (jnp.minimum(zf, 0.0) - jnp.log1p(jnp.exp(-jnp.abs(zf))))
    logf = jnp.maximum(loglb, bterm) + jnp.log1p(jnp.exp(-jnp.abs(loglb - bterm)))
    k = 1.0 - jnp.exp(logf)
    v = zi
    bcum = _mm_exact_lhs(tri, logf)
    rows = lax.broadcasted_iota(jnp.int32, (SUB, 1), 0)
    outs = []
    for i in range(c // SUB):
        lo = i * SUB
        bi = bcum[lo:lo + SUB]
        qi = q[lo:lo + SUB]
        ki = k[lo:lo + SUB]
        vi = v[lo:lo + SUB]
        o = _mm(qi * jnp.exp(bi), st, NT)
        if i > 0:
            b0 = bcum[lo - 1:lo]
            qt = qi * jnp.exp(bi - b0)
            kt = k[:lo] * jnp.exp(b0 - bcum[:lo])
            o = o + _mm(_mm(qt, kt, NT), v[:lo])
        for s in range(SUB):
            dec = jnp.exp(jnp.minimum(bi - bi[s:s + 1], 0.0))
            col = jnp.sum(qi * ki[s:s + 1] * dec, axis=-1, keepdims=True)
            o = o + jnp.where(rows >= s, col, 0.0) * vi[s:s + 1]
        outs.append(o)
    o = outs[0] if len(outs) == 1 else jnp.concatenate(outs, axis=0)
    b_last = bcum[c - 1:c]
    st_new = st * jnp.exp(b_last) + _mm(v, k * jnp.exp(b_last - bcum), TN)
    on = o * lax.rsqrt(jnp.mean(o * o, axis=-1, keepdims=True) + NORM_EPS) * nw
    return on * _silu(zg), st_new


def _hgrn_kernel(c, hg_ref, lb_ref, nw_ref, s0_ref, o_ref, sfin_ref):
    t = pl.program_id(1)

    @pl.when(t == 0)
    def _():
        sfin_ref[...] = s0_ref[...]

    tri = _tri_incl_bf16(c)
    qk = HG_HEADS * HG_DK
    for ci in range(hg_ref.shape[0] // c):
        r0 = ci * c
        for h in range(HG_HEADS):
            zq = hg_ref[r0:r0 + c, h * HG_DK:(h + 1) * HG_DK]
            zf = hg_ref[r0:r0 + c, qk + h * HG_DK:qk + (h + 1) * HG_DK]
            zi = hg_ref[r0:r0 + c, 2 * qk + h * HG_DV:2 * qk + (h + 1) * HG_DV]
            zg = hg_ref[r0:r0 + c, 2 * qk + HG_WIDTH + h * HG_DV:2 * qk + HG_WIDTH + (h + 1) * HG_DV]
            sl = slice(h * HG_DK, (h + 1) * HG_DK)
            out, st_new = _hgrn_chunk(zq, zf, zi, zg, sfin_ref[h], lb_ref[0:1, sl], lb_ref[1:2, sl],
                                      nw_ref[0:1, h * HG_DV:(h + 1) * HG_DV], tri)
            sfin_ref[h] = st_new
            o_ref[r0:r0 + c, h * HG_DV:(h + 1) * HG_DV] = out


def _hgrn(hgraw, lbvec, nw, s0t, ch, c):
    b, l, _ = hgraw.shape
    return pl.pallas_call(
        functools.partial(_hgrn_kernel, c),
        grid=(b, l // ch),
        in_specs=[
            pl.BlockSpec((None, ch, HG_PROJ), lambda i, j: (i, j, 0)),
            pl.BlockSpec(lbvec.shape, lambda i, j: (0, 0)),
            pl.BlockSpec(nw.shape, lambda i, j: (0, 0)),
            pl.BlockSpec((None, HG_HEADS, HG_DV, HG_DK), lambda i, j: (i, 0, 0, 0)),
        ],
        out_specs=[
            pl.BlockSpec((None, ch, HG_WIDTH), lambda i, j: (i, j, 0)),
            pl.BlockSpec((None, HG_HEADS, HG_DV, HG_DK), lambda i, j: (i, 0, 0, 0)),
        ],
        out_shape=[
            jax.ShapeDtypeStruct((b, l, HG_WIDTH), F32),
            jax.ShapeDtypeStruct((b, HG_HEADS, HG_DV, HG_DK), F32),
        ],
        compiler_params=_params(("parallel", "arbitrary")),
        name="hgrn",
    )(hgraw, lbvec, nw, s0t)


RW_PASSES_A = 1
RW_PASSES_B = 1


def _rwkv_chunk_matrices(c, r, lw, km, v, kap, beta, tri, strict, incl, same_sub):
    pa = RW_PASSES_A
    hs = range(RW_HEADS)
    sl = [slice(h * RW_HEAD, (h + 1) * RW_HEAD) for h in hs]
    cum = _mm_exact_lhs(tri, lw)
    c_last = cum[c - 1:c]
    e_neg = jnp.exp(-cum)
    e_hat = jnp.exp(c_last - cum)
    kbar = kap * jnp.exp(cum - lw)
    rbar = r * jnp.exp(cum)
    bneg = beta * e_neg
    kneg = km * e_neg
    bhat = beta * e_hat
    khat = km * e_hat
    e_last = jnp.exp(c_last)
    mask_kq = jnp.concatenate([strict, incl], axis=0)
    lhs = [jnp.concatenate([kbar[:, sl[h]], rbar[:, sl[h]]], axis=0) for h in hs]
    rhs = [jnp.concatenate([bneg[:, sl[h]], kneg[:, sl[h]]], axis=0) for h in hs]
    gmat = [_mm(lhs[h], rhs[h], NT, passes=pa) for h in hs]
    aab = [jnp.where(strict, gmat[h][:c, :c], 0.0) for h in hs]
    aqb = [jnp.where(incl, gmat[h][c:, :c], 0.0) for h in hs]
    akq = [jnp.where(mask_kq, gmat[h][:, c:], 0.0) for h in hs]
    av = [_mm(akq[h], v[:, sl[h]], passes=pa) for h in hs]
    d1 = [jnp.where(same_sub, aab[h], 0.0) for h in hs]
    if c == SUB:
        z = [jnp.concatenate([kbar[:, sl[h]], av[h][:c]], axis=1) for h in hs]
    else:
        z = [jnp.concatenate([aab[h] - d1[h], kbar[:, sl[h]], av[h][:c]], axis=1) for h in hs]
    d2 = [_mm(d1[h], d1[h], passes=pa) for h in hs]
    z = [z[h] - _mm(d1[h], z[h], passes=pa) for h in hs]
    d4 = [_mm(d2[h], d2[h], passes=pa) for h in hs]
    z = [z[h] + _mm(d2[h], z[h], passes=pa) for h in hs]
    d8 = [_mm(d4[h], d4[h], passes=pa) for h in hs]
    z = [z[h] + _mm(d4[h], z[h], passes=pa) for h in hs]
    z = [z[h] + _mm(d8[h], z[h], passes=pa) for h in hs]
    if c == SUB:
        x = z
    else:
        f = [z[h][:, :c] for h in hs]
        x = [z[h][:, c:] for h in hs]
        fx = [_mm(f[h], x[h], passes=pa) for h in hs]
        x = [x[h] + _mm(f[h], fx[h], passes=pa) for h in hs]
        x = [x[h] - _mm(f[h], x[h], passes=pa) for h in hs]
    qb = [_mm(aqb[h], x[h], passes=pa) for h in hs]
    rt = [rbar[:, sl[h]] - qb[h][:, :RW_HEAD] for h in hs]
    yc = [av[h][c:] - qb[h][:, RW_HEAD:] for h in hs]
    ktb = [_mm(x[h][:, :RW_HEAD], bhat[:, sl[h]], TN, passes=pa) for h in hs]
    nc = [_mm(jnp.concatenate([v[:, sl[h]], x[h][:, RW_HEAD:]], axis=0),
              jnp.concatenate([khat[:, sl[h]], -bhat[:, sl[h]]], axis=0), TN, passes=pa) for h in hs]
    return rt, yc, ktb, nc, e_last


def _rwkv_kernel(c, rw_ref, vec_ref, s0_ref, o_ref, sfin_ref):
    t = pl.program_id(1)

    @pl.when(t == 0)
    def _():
        sfin_ref[...] = s0_ref[...]

    tri = _tri_incl_bf16(c)
    row = lax.broadcasted_iota(jnp.int32, (c, c), 0)
    col = lax.broadcasted_iota(jnp.int32, (c, c), 1)
    strict = row > col
    incl = row >= col
    same_sub = (row // SUB) == (col // SUB)
    w = RW_WIDTH
    hs = range(RW_HEADS)
    sl = [slice(h * RW_HEAD, (h + 1) * RW_HEAD) for h in hs]
    state = [sfin_ref[h] for h in hs]
    for ci in range(rw_ref.shape[0] // c):
        r0 = ci * c
        r = rw_ref[r0:r0 + c, 0 * w:1 * w]
        lw = rw_ref[r0:r0 + c, 1 * w:2 * w]
        km = rw_ref[r0:r0 + c, 2 * w:3 * w]
        v = rw_ref[r0:r0 + c, 3 * w:4 * w]
        kap = rw_ref[r0:r0 + c, 4 * w:5 * w]
        beta = rw_ref[r0:r0 + c, 5 * w:6 * w]
        gate = rw_ref[r0:r0 + c, 6 * w:7 * w]
        rt, yc, ktb, nc, e_last = _rwkv_chunk_matrices(c, r, lw, km, v, kap, beta, tri, strict, incl, same_sub)
        y = [_mm(rt[h], state[h], NT, passes=RW_PASSES_B) + yc[h] for h in hs]
        state = [state[h] * e_last[:, sl[h]] - _mm(state[h], ktb[h], passes=RW_PASSES_B) + nc[h] for h in hs]
        bonus = r * km * vec_ref[2:3, :]
        for h in hs:
            mean = jnp.mean(y[h], axis=-1, keepdims=True)
            yd = y[h] - mean
            var = jnp.mean(yd * yd, axis=-1, keepdims=True)
            yn = yd * lax.rsqrt(var + RW_GN_EPS) * vec_ref[0:1, sl[h]] + vec_ref[1:2, sl[h]]
            yn = yn + jnp.sum(bonus[:, sl[h]], axis=-1, keepdims=True) * v[:, sl[h]]
            o_ref[r0:r0 + c, sl[h]] = yn * gate[:, sl[h]]
    for h in hs:
        sfin_ref[h] = state[h]


def _rwkv(rwp, vec, s0, ch, c):
    b, l, _ = rwp.shape
    return pl.pallas_call(
        functools.partial(_rwkv_kernel, c),
        grid=(b, l // ch),
        in_specs=[
            pl.BlockSpec((None, ch, RW_PACK), lambda i, j: (i, j, 0)),
            pl.BlockSpec(vec.shape, lambda i, j: (0, 0)),
            pl.BlockSpec((None, RW_HEADS, RW_HEAD, RW_HEAD), lambda i, j: (i, 0, 0, 0)),
        ],
        out_specs=[
            pl.BlockSpec((None, ch, RW_WIDTH), lambda i, j: (i, j, 0)),
            pl.BlockSpec((None, RW_HEADS, RW_HEAD, RW_HEAD), lambda i, j: (i, 0, 0, 0)),
        ],
        out_shape=[
            jax.ShapeDtypeStruct((b, l, RW_WIDTH), F32),
            jax.ShapeDtypeStruct((b, RW_HEADS, RW_HEAD, RW_HEAD), F32),
        ],
        compiler_params=_params(("parallel", "arbitrary")),
        name="rwkv",
    )(rwp, vec, s0)


def _postmix_kernel(ohg_ref, yrw_ref, x_ref, bvec_ref, nw_ref, wout_ref, wr_ref, br_ref,
                    xn_ref, h2_ref, route_ref):
    m = (jnp.dot(ohg_ref[...].astype(BF16), wout_ref[:HG_WIDTH, :], preferred_element_type=F32)
         + jnp.dot(yrw_ref[...].astype(BF16), wout_ref[HG_WIDTH:, :], preferred_element_type=F32))
    xn = x_ref[...] + bvec_ref[0:1, :] * m
    xn_ref[...] = xn
    h2 = (xn * lax.rsqrt(jnp.mean(xn * xn, axis=-1, keepdims=True) + NORM_EPS) * nw_ref[...]
          * (1.0 + bvec_ref[2:3, :]) + bvec_ref[1:2, :])
    h2_ref[...] = h2

    logits = _mm(h2, wr_ref[...]) + br_ref[...]
    lane = lax.broadcasted_iota(jnp.int32, logits.shape, 1)
    neg = jnp.float32(-jnp.inf)
    big = jnp.int32(ROUTE_W)
    is_g = lane < N_GROUPS
    gl = jnp.where(is_g, logits, neg)
    gmax = jnp.max(gl, axis=-1, keepdims=True)
    grp = jnp.min(jnp.where(gl == gmax, lane, big), axis=-1, keepdims=True)
    p_grp = 1.0 / jnp.sum(jnp.where(is_g, jnp.exp(gl - gmax), 0.0), axis=-1, keepdims=True)
    e_lo = N_GROUPS + grp * EXPERTS_PER_GROUP
    in_grp = (lane >= e_lo) & (lane < e_lo + EXPERTS_PER_GROUP)
    el = jnp.where(in_grp, logits, neg)
    m1 = jnp.max(el, axis=-1, keepdims=True)
    i1 = jnp.min(jnp.where(el == m1, lane, big), axis=-1, keepdims=True)
    el2 = jnp.where(lane == i1, neg, el)
    m2 = jnp.max(el2, axis=-1, keepdims=True)
    i2 = jnp.min(jnp.where(el2 == m2, lane, big), axis=-1, keepdims=True)
    e21 = jnp.exp(m2 - m1)
    p1 = 1.0 / (1.0 + e21)
    p2 = e21 / (1.0 + e21)
    out = jnp.where(lane == 0, (i1 - N_GROUPS).astype(F32),
                    jnp.where(lane == 1, (i2 - N_GROUPS).astype(F32),
                              jnp.where(lane == 2, p_grp * p1,
                                        jnp.where(lane == 3, p_grp * p2, 0.0))))
    route_ref[...] = out[:, :8]


def _postmix(ohg, yrw, x, bvec, nw, wout, wr, br, tm):
    b, l, d = x.shape
    const = lambda shape: pl.BlockSpec(shape, lambda i, j: tuple(0 for _ in shape))
    return pl.pallas_call(
        _postmix_kernel,
        grid=(b, l // tm),
        in_specs=[
            pl.BlockSpec((None, tm, HG_WIDTH), lambda i, j: (i, j, 0)),
            pl.BlockSpec((None, tm, RW_WIDTH), lambda i, j: (i, j, 0)),
            pl.BlockSpec((None, tm, d), lambda i, j: (i, j, 0)),
            pl.BlockSpec((None, 8, d), lambda i, j: (i, 0, 0)),
            const(nw.shape), const(wout.shape), const(wr.shape), const(br.shape),
        ],
        out_specs=[
            pl.BlockSpec((None, tm, d), lambda i, j: (i, j, 0)),
            pl.BlockSpec((None, tm, d), lambda i, j: (i, j, 0)),
            pl.BlockSpec((None, tm, 8), lambda i, j: (i, j, 0)),
        ],
        out_shape=[
            jax.ShapeDtypeStruct((b, l, d), F32),
            jax.ShapeDtypeStruct((b, l, d), F32),
            jax.ShapeDtypeStruct((b, l, 8), F32),
        ],
        compiler_params=_params(("parallel", "parallel")),
        name="postmix",
    )(ohg, yrw, x, bvec, nw, wout, wr, br)


def _expert_kernel(be_ref, nu_ref, xs_ref, wg_ref, wu_ref, wd_ref, o_ref):
    i = pl.program_id(0)

    @pl.when(i < nu_ref[0])
    def _():
        xb = xs_ref[...].astype(BF16)
        hg = jnp.dot(xb, wg_ref[...].astype(BF16), preferred_element_type=F32)
        hu = jnp.dot(xb, wu_ref[...].astype(BF16), preferred_element_type=F32)
        hdn = (_silu(hg) * hu).astype(BF16)
        o_ref[...] = jnp.dot(hdn, wd_ref[...].astype(BF16), preferred_element_type=F32)

    @pl.when(i >= nu_ref[0])
    def _():
        o_ref[...] = jnp.zeros_like(o_ref)


def _experts(xs, block_e, n_used, wg, wu, wd, layer, blk):
    p, d = xs.shape
    nb = p // blk
    grid_spec = pltpu.PrefetchScalarGridSpec(
        num_scalar_prefetch=2,
        grid=(nb,),
        in_specs=[
            pl.BlockSpec((blk, d), lambda i, be, nu: (i, 0)),
            pl.BlockSpec((None, None, d, D_EXPERT), lambda i, be, nu: (layer, be[i], 0, 0)),
            pl.BlockSpec((None, None, d, D_EXPERT), lambda i, be, nu: (layer, be[i], 0, 0)),
            pl.BlockSpec((None, None, D_EXPERT, d), lambda i, be, nu: (layer, be[i], 0, 0)),
        ],
        out_specs=pl.BlockSpec((blk, d), lambda i, be, nu: (i, 0)),
    )
    return pl.pallas_call(
        _expert_kernel,
        grid_spec=grid_spec,
        out_shape=jax.ShapeDtypeStruct((p, d), F32),
        compiler_params=_params(("arbitrary",)),
        name="experts",
    )(block_e, n_used, xs, wg, wu, wd)


def _combine_kernel(final, xn_ref, y0_ref, y1_ref, route_ref, bvec_ref, fw_ref, o_ref):
    rt = route_ref[...]
    moe = rt[:, 2:3] * y0_ref[...] + rt[:, 3:4] * y1_ref[...]
    x = xn_ref[...] + bvec_ref[3:4, :] * moe
    if final:
        x = x * lax.rsqrt(jnp.mean(x * x, axis=-1, keepdims=True) + NORM_EPS) * fw_ref[...]
    o_ref[...] = x


def _combine(xn, y0, y1, route, bvec, fw, final, tm):
    b, l, d = xn.shape
    row = lambda w: pl.BlockSpec((None, tm, w), lambda i, j: (i, j, 0))
    return pl.pallas_call(
        functools.partial(_combine_kernel, final),
        grid=(b, l // tm),
        in_specs=[row(d), row(d), row(d), row(8),
                  pl.BlockSpec((None, 8, d), lambda i, j: (i, 0, 0)),
                  pl.BlockSpec((1, d), lambda i, j: (0, 0))],
        out_specs=row(d),
        out_shape=jax.ShapeDtypeStruct((b, l, d), F32),
        compiler_params=_params(("parallel", "parallel")),
        name="combine",
    )(xn, y0, y1, route, bvec, fw)


def _dispatch(route, blk):
    t = route.shape[0]
    n = t * TOP_K
    flat_e = route[:, :TOP_K].astype(jnp.int32).reshape(n)
    experts = jnp.arange(N_EXPERTS, dtype=jnp.int32)
    onehot = (flat_e[:, None] == experts[None, :]).astype(jnp.int32)
    csum = jnp.cumsum(onehot, axis=0)
    counts = csum[-1]
    padded = ((counts + blk - 1) // blk) * blk
    start = jnp.cumsum(counts) - counts
    pend = jnp.cumsum(padded)
    pstart = pend - padded
    slot = jnp.sum(onehot * (csum - 1 + pstart[None, :]), axis=1)
    nb = -(-(n + N_EXPERTS * (blk - 1)) // blk)
    block_e = jnp.minimum(jnp.sum((pend // blk)[None, :] <= jnp.arange(nb, dtype=jnp.int32)[:, None], axis=1),
                          N_EXPERTS - 1).astype(jnp.int32)
    n_used = (pend[-1] // blk).astype(jnp.int32).reshape(1)
    order = jnp.argsort(flat_e).astype(jnp.int32)
    row_e = jnp.repeat(block_e, blk)
    row = jnp.arange(nb * blk, dtype=jnp.int32)
    off = row - pstart[row_e]
    src = jnp.clip(start[row_e] + off, 0, n - 1)
    buf_tok = jnp.where(off < counts[row_e], order[src] // TOP_K, 0)
    return buf_tok, slot.reshape(t, TOP_K), block_e, n_used


def _layer_weights(p, l):
    f = lambda name: p[name][l]
    zero_d = jnp.zeros((D_MODEL,), F32)
    has_vres = l > 0
    vecd = jnp.stack([f('norm_mix_w'), p['rw_mu_lora'][l, 0], p['rw_mu_lora'][l, 1], p['rw_mu_lora'][l, 2],
                      p['rw_mu_vres'][l - 1] if has_vres else zero_d, zero_d, zero_d, zero_d])
    zero_w = jnp.zeros((RW_WIDTH,), F32)
    vec5 = jnp.stack([f('rw_w0'), f('rw_a0'), p['rw_v0'][l - 1] if has_vres else zero_w,
                      f('rw_kk'), f('rw_ka'), zero_w, zero_w, zero_w])
    loras = [f('rw_w1'), f('rw_a1'), f('rw_g1')] + ([p['rw_v1'][l - 1]] if has_vres else [])
    loras += [f('rw_w2'), f('rw_a2'), f('rw_g2')] + ([p['rw_v2'][l - 1]] if has_vres else [])
    loras = [w.astype(BF16) for w in loras]
    rwvec = jnp.stack([f('rw_ln_w'), f('rw_ln_b'), f('rw_rk').reshape(RW_WIDTH),
                       zero_w, zero_w, zero_w, zero_w, zero_w])
    wr = jnp.concatenate([f('router_g_w'),
                          jnp.transpose(f('router_e_w'), (1, 0, 2)).reshape(D_MODEL, N_EXPERTS),
                          jnp.zeros((D_MODEL, ROUTE_W - N_GROUPS - N_EXPERTS), F32)], axis=1)
    br = jnp.concatenate([f('router_g_b'), f('router_e_b').reshape(N_EXPERTS),
                          jnp.zeros((ROUTE_W - N_GROUPS - N_EXPERTS,), F32)]).reshape(1, ROUTE_W)
    return dict(
        vecd=vecd, vec5=vec5, mup=f('rw_mu_proj').reshape(1, RW_PROJ), win=f('w_in').astype(BF16),
        loras=loras, rwvec=rwvec, hg_nw=f('hg_norm_w').reshape(1, HG_WIDTH),
        nfw=f('norm_ffn_w').reshape(1, D_MODEL), wout=f('w_out').astype(BF16), wr=wr, br=br)


def _trunk(x, mod, s_hg, s_rw, h_prev, lw_all, expert_w, lbvecs, final_w):
    b, l, d = x.shape
    depth = len(lw_all)
    tm = min(l, 256)
    ch = min(l, 256)
    c = min(l, CHUNK)
    blk = 256 if b * l * TOP_K >= 8192 else 32
    head = lax.broadcasted_iota(jnp.int32, (RW_WIDTH, RW_WIDTH), 0) // RW_HEAD
    seg = (head == head.T).astype(BF16)
    new_hg, new_rw, new_shift = [], [], []
    rwp_first = None
    zrow = jnp.zeros((b, d), F32)
    for li in range(depth):
        w = lw_all[li]
        sh1, sc1, g1, sh2, sc2, g2 = jnp.split(mod[li], 6, axis=-1)
        bvec1 = jnp.stack([sh1, sc1, h_prev[li], zrow, zrow, zrow, zrow, zrow], axis=1)
        bvec2 = jnp.stack([g1, sh2, sc2, g2, zrow, zrow, zrow, zrow], axis=1)
        hgraw, rwp, hlast = _premix(x, bvec1, w['vecd'], w['vec5'], w['mup'], w['win'], w['loras'],
                                    seg, rwp_first if li > 0 else None, tm)
        if li == 0:
            rwp_first = rwp
        ohg, hg_t = _hgrn(hgraw, lbvecs[li], w['hg_nw'], jnp.swapaxes(s_hg[li], -1, -2), ch, c)
        yrw, rw_fin = _rwkv(rwp, w['rwvec'], s_rw[li], ch, c)
        xn, h2, route = _postmix(ohg, yrw, x, bvec2, w['nfw'], w['wout'], w['wr'], w['br'], tm)
        t = b * l
        buf_tok, slot, block_e, n_used = _dispatch(route.reshape(t, 8), blk)
        xs = h2.reshape(t, d).at[buf_tok].get(mode='promise_in_bounds')
        yb = _experts(xs, block_e, n_used, *expert_w, li, blk)
        y0 = yb.at[slot[:, 0]].get(mode='promise_in_bounds').reshape(b, l, d)
        y1 = yb.at[slot[:, 1]].get(mode='promise_in_bounds').reshape(b, l, d)
        x = _combine(xn, y0, y1, route, bvec2, final_w, li == depth - 1, tm)
        new_hg.append(jnp.swapaxes(hg_t, -1, -2))
        new_rw.append(rw_fin)
        new_shift.append(hlast.reshape(b, d))
    return x, jnp.stack(new_hg), jnp.stack(new_rw), jnp.stack(new_shift)


def kernel(x_prompt, x_sample, state_hgrn, state_rwkv, state_shift, c_prompt, c_sample, w_ada, b_ada, norm_mix_w, norm_ffn_w, w_in, w_out, hg_lb_logits, hg_norm_w, rw_mu_proj, rw_mu_lora, rw_w0, rw_w1, rw_w2, rw_a0, rw_a1, rw_a2, rw_g1, rw_g2, rw_mu_vres, rw_v0, rw_v1, rw_v2, rw_kk, rw_ka, rw_rk, rw_ln_w, rw_ln_b, router_g_w, router_g_b, router_e_w, router_e_b, w_gate, w_up, w_down, final_norm_w):
    p = dict(norm_mix_w=norm_mix_w, norm_ffn_w=norm_ffn_w, w_in=w_in, w_out=w_out, hg_norm_w=hg_norm_w,
             rw_mu_proj=rw_mu_proj, rw_mu_lora=rw_mu_lora, rw_w0=rw_w0, rw_w1=rw_w1, rw_w2=rw_w2,
             rw_a0=rw_a0, rw_a1=rw_a1, rw_a2=rw_a2, rw_g1=rw_g1, rw_g2=rw_g2,
             rw_mu_vres=rw_mu_vres, rw_v0=rw_v0, rw_v1=rw_v1, rw_v2=rw_v2,
             rw_kk=rw_kk, rw_ka=rw_ka, rw_rk=rw_rk, rw_ln_w=rw_ln_w, rw_ln_b=rw_ln_b,
             router_g_w=router_g_w, router_g_b=router_g_b, router_e_w=router_e_w,
             router_e_b=router_e_b, w_gate=w_gate, w_up=w_up, w_down=w_down)
    depth = w_in.shape[0]
    bp = x_prompt.shape[0]
    bs = x_sample.shape[0]
    d = x_prompt.shape[-1]
    lw_all = [_layer_weights(p, l) for l in range(depth)]
    lb = jnp.cumsum(jax.nn.softmax(hg_lb_logits.astype(F32), axis=0), axis=0)
    lb = lb - lb[0]
    zpad = jnp.zeros((6, lb.shape[1]), F32)
    lbvecs = [jnp.concatenate([jnp.log(lb[l])[None], jnp.log1p(-lb[l])[None], zpad], axis=0)
              for l in range(depth)]
    rows = -(-(bp + bs) // 8) * 8
    c_all = jnp.concatenate([c_prompt, c_sample, jnp.zeros((rows - bp - bs, d), F32)], axis=0)
    mod = _ada_mod(c_all, w_ada, b_ada)
    fw = final_norm_w.reshape(1, d)
    zero_hg = jnp.zeros((depth, bp) + state_hgrn.shape[2:], F32)
    zero_rw = jnp.zeros((depth, bp) + state_rwkv.shape[2:], F32)
    zero_sh = jnp.zeros((depth, bp, d), F32)
    expert_w = (w_gate, w_up, w_down)
    y_p, hg_p, rw_p, sh_p = _trunk(x_prompt, mod[:, :bp], zero_hg, zero_rw, zero_sh, lw_all, expert_w,
                                   lbvecs, fw)
    y_s, hg_s, rw_s, sh_s = _trunk(x_sample, mod[:, bp:bp + bs], state_hgrn, state_rwkv, state_shift,
                                   lw_all, expert_w, lbvecs, fw)
    return (y_p, y_s, hg_p, rw_p, sh_p, hg_s, rw_s, sh_s)
```

```python
import functools

import jax
import jax.numpy as jnp
from jax import lax
from jax.experimental import pallas as pl
from jax.experimental.pallas import tpu as pltpu

F32 = jnp.float32
BF16 = jnp.bfloat16

D_MODEL = 1024
HG_HEADS = 4
HG_DK = 128
HG_DV = 128
HG_WIDTH = HG_HEADS * HG_DV
RW_HEADS = 8
RW_HEAD = 64
RW_WIDTH = RW_HEADS * RW_HEAD
HG_PROJ = 2 * HG_HEADS * HG_DK + 2 * HG_WIDTH
RW_PROJ = 3 * RW_WIDTH
IN_WIDTH = HG_PROJ + RW_PROJ
N_GROUPS = 4
EXPERTS_PER_GROUP = 8
N_EXPERTS = N_GROUPS * EXPERTS_PER_GROUP
TOP_K = 2
D_EXPERT = 512
NORM_EPS = 1e-6
RW_GN_EPS = 64e-5

CHUNK = 64
SUB = 16
HG_SUB = 8
ROUTE_W = 128
RW_PACK = 7 * RW_WIDTH
VMEM_LIMIT = 56 * 1024 * 1024

NN = (((1,), (0,)), ((), ()))
NT = (((1,), (1,)), ((), ()))
TN = (((0,), (0,)), ((), ()))


def _split2(a):
    hi = a.astype(BF16)
    lo = (a - hi.astype(F32)).astype(BF16)
    return hi, lo


def _mm(a, b, dims=NN, passes=1):
    if passes == 1:
        return lax.dot_general(a.astype(BF16), b.astype(BF16), dims, preferred_element_type=F32)
    a1, a2 = _split2(a)
    b1, b2 = _split2(b)
    out = lax.dot_general(a1, b1, dims, preferred_element_type=F32)
    out = out + lax.dot_general(a1, b2, dims, preferred_element_type=F32)
    return out + lax.dot_general(a2, b1, dims, preferred_element_type=F32)


def _mm_exact_lhs(a_bf, b):
    b1 = b.astype(BF16)
    r = b - b1.astype(F32)
    b2 = r.astype(BF16)
    b3 = (r - b2.astype(F32)).astype(BF16)
    out = jnp.dot(a_bf, b1, preferred_element_type=F32)
    out = out + jnp.dot(a_bf, b2, preferred_element_type=F32)
    return out + jnp.dot(a_bf, b3, preferred_element_type=F32)


def _sigmoid(x):
    return 1.0 / (1.0 + jnp.exp(-x))


def _silu(x):
    return x * _sigmoid(x)


def _softplus(x):
    return jnp.maximum(x, 0.0) + jnp.log1p(jnp.exp(-jnp.abs(x)))


def _tri_incl_bf16(n):
    r = lax.broadcasted_iota(jnp.int32, (n, n), 0)
    c = lax.broadcasted_iota(jnp.int32, (n, n), 1)
    return (r >= c).astype(BF16)


def _params(sem):
    return pltpu.CompilerParams(dimension_semantics=sem, vmem_limit_bytes=VMEM_LIMIT)


def _ada_kernel(c_ref, w_ref, b_ref, o_ref):
    cs = _silu(c_ref[...])
    o_ref[...] = _mm(cs, w_ref[...]) + b_ref[...]


def _ada_mod(c_all, w_ada, b_ada):
    depth, d, n = w_ada.shape
    rows = c_all.shape[0]
    tn = 1536
    return pl.pallas_call(
        _ada_kernel,
        grid=(depth, n // tn),
        in_specs=[
            pl.BlockSpec((rows, d), lambda l, j: (0, 0)),
            pl.BlockSpec((None, d, tn), lambda l, j: (l, 0, j)),
            pl.BlockSpec((None, 1, tn), lambda l, j: (l, 0, j)),
        ],
        out_specs=pl.BlockSpec((None, rows, tn), lambda l, j: (l, 0, j)),
        out_shape=jax.ShapeDtypeStruct((depth, rows, n), F32),
        compiler_params=_params(("parallel", "parallel")),
        name="ada_mod",
    )(c_all, w_ada, b_ada.reshape(depth, 1, n))


def _premix_kernel(has_vres, *refs):
    if has_vres:
        (x_ref, bvec_ref, vecd_ref, vec5_ref, mup_ref, win_ref, w1_ref, a1_ref, g1_ref, v1_ref,
         w2_ref, a2_ref, g2_ref, v2_ref, seg_ref, vfirst_ref,
         hg_ref, rw_ref, hlast_ref, hcar, pcar) = refs
    else:
        (x_ref, bvec_ref, vecd_ref, vec5_ref, mup_ref, win_ref, w1_ref, a1_ref, g1_ref,
         w2_ref, a2_ref, g2_ref, seg_ref,
         hg_ref, rw_ref, hlast_ref, hcar, pcar) = refs
    t = pl.program_id(1)
    tm = x_ref.shape[0]
    w_rw = win_ref[:, HG_PROJ:]

    @pl.when(t == 0)
    def _():
        hp = jnp.broadcast_to(bvec_ref[2:3, :], (8, D_MODEL))
        hcar[...] = hp
        pcar[...] = _mm(hp, w_rw)

    x = x_ref[...]
    nw = vecd_ref[0:1, :]
    sh = bvec_ref[0:1, :]
    sc = bvec_ref[1:2, :]
    xn = x * lax.rsqrt(jnp.mean(x * x, axis=-1, keepdims=True) + NORM_EPS) * nw
    h = xn * (1.0 + sc) + sh

    hb = h.astype(BF16)
    hg_ref[...] = jnp.dot(hb, win_ref[:, :HG_PROJ], preferred_element_type=F32)
    p_cur = jnp.dot(hb, w_rw, preferred_element_type=F32)

    first = lax.broadcasted_iota(jnp.int32, (tm, 1), 0) == 0
    h_shift = jnp.where(first, hcar[0:1, :], pltpu.roll(h, 1, 0))
    p_shift = jnp.where(first, pcar[0:1, :], pltpu.roll(p_cur, 1, 0))
    hcar[0:1, :] = h[tm - 1:tm, :]
    pcar[0:1, :] = p_cur[tm - 1:tm, :]
    hlast_ref[...] = h[tm - 1:tm, :]

    pm = p_cur + (p_shift - p_cur) * mup_ref[...]
    r = pm[:, :RW_WIDTH]
    k = pm[:, RW_WIDTH:2 * RW_WIDTH]
    v = pm[:, 2 * RW_WIDTH:]

    dh = h_shift - h
    xw = h + dh * vecd_ref[1:2, :]
    xa = h + dh * vecd_ref[2:3, :]
    xg = h + dh * vecd_ref[3:4, :]
    w_pre = vec5_ref[0:1, :] + _mm(jnp.tanh(_mm(xw, w1_ref[...])), w2_ref[...])
    logw = -jnp.exp(-_softplus(-w_pre) - 0.5)
    a = _sigmoid(vec5_ref[1:2, :] + _mm(_mm(xa, a1_ref[...]), a2_ref[...]))
    g = _mm(_sigmoid(_mm(xg, g1_ref[...])), g2_ref[...])
    if has_vres:
        xv = h + dh * vecd_ref[4:5, :]
        nu = _sigmoid(vec5_ref[2:3, :] + _mm(_mm(xv, v1_ref[...]), v2_ref[...]))
        v = v + (vfirst_ref[...] - v) * nu

    kk = k * vec5_ref[3:4, :]
    ss = _mm(kk * kk, seg_ref[...], passes=3)
    kappa = kk / jnp.maximum(jnp.sqrt(ss), 1e-12)
    k_mod = k * (1.0 + (a - 1.0) * vec5_ref[4:5, :])

    rw_ref[:, 0 * RW_WIDTH:1 * RW_WIDTH] = r
    rw_ref[:, 1 * RW_WIDTH:2 * RW_WIDTH] = logw
    rw_ref[:, 2 * RW_WIDTH:3 * RW_WIDTH] = k_mod
    rw_ref[:, 3 * RW_WIDTH:4 * RW_WIDTH] = v
    rw_ref[:, 4 * RW_WIDTH:5 * RW_WIDTH] = kappa
    rw_ref[:, 5 * RW_WIDTH:6 * RW_WIDTH] = kappa * a
    rw_ref[:, 6 * RW_WIDTH:7 * RW_WIDTH] = g


def _premix(x, bvec, vecd, vec5, mup, win, loras, seg, vfirst, tm):
    b, l, d = x.shape
    has_vres = vfirst is not None
    const = lambda shape: pl.BlockSpec(shape, lambda i, j: tuple(0 for _ in shape))
    in_specs = [
        pl.BlockSpec((None, tm, d), lambda i, j: (i, j, 0)),
        pl.BlockSpec((None, 8, d), lambda i, j: (i, 0, 0)),
        const(vecd.shape), const(vec5.shape), const(mup.shape), const(win.shape),
    ]
    in_specs += [const(w.shape) for w in loras]
    in_specs.append(const(seg.shape))
    args = [x, bvec, vecd, vec5, mup, win, *loras, seg]
    if has_vres:
        in_specs.append(pl.BlockSpec((None, tm, RW_WIDTH), lambda i, j: (i, j, 3)))
        args.append(vfirst)
    return pl.pallas_call(
        functools.partial(_premix_kernel, has_vres),
        grid=(b, l // tm),
        in_specs=in_specs,
        out_specs=[
            pl.BlockSpec((None, tm, HG_PROJ), lambda i, j: (i, j, 0)),
            pl.BlockSpec((None, tm, RW_PACK), lambda i, j: (i, j, 0)),
            pl.BlockSpec((None, 1, d), lambda i, j: (i, 0, 0)),
        ],
        out_shape=[
            jax.ShapeDtypeStruct((b, l, HG_PROJ), F32),
            jax.ShapeDtypeStruct((b, l, RW_PACK), F32),
            jax.ShapeDtypeStruct((b, 1, d), F32),
        ],
        scratch_shapes=[pltpu.VMEM((8, d), F32), pltpu.VMEM((8, RW_PROJ), F32)],
        compiler_params=_params(("parallel", "arbitrary")),
        name="premix",
    )(*args)


def _hgrn_chunk(zq, zf, zi, zg, st, loglb, log1mlb, nw, tri):
    c = zq.shape[0]
    q = _silu(zq)
    bterm = log1mlb + (jnp.minimum(zf, 0.0) - jnp.log1p(jnp.exp(-jnp.abs(zf))))
    logf = jnp.maximum(loglb, bterm) + jnp.log1p(jnp.exp(-jnp.abs(loglb - bterm)))
    k = 1.0 - jnp.exp(logf)
    v = zi
    bcum = _mm_exact_lhs(tri, logf)
    rows = lax.broadcasted_iota(jnp.int32, (HG_SUB, 1), 0)
    o_state = _mm(q * jnp.exp(bcum), st, NT)
    outs = []
    for i in range(c // HG_SUB):
        lo = i * HG_SUB
        bi = bcum[lo:lo + HG_SUB]
        qi = q[lo:lo + HG_SUB]
        ki = k[lo:lo + HG_SUB]
        vi = v[lo:lo + HG_SUB]
        o = o_state[lo:lo + HG_SUB]
        if i > 0:
            b0 = bcum[lo - 1:lo]
            qt = qi * jnp.exp(bi - b0)
            kt = k[:lo] * jnp.exp(b0 - bcum[:lo])
            o = o + _mm(_mm(qt, kt, NT), v[:lo])
        for s in range(HG_SUB):
            dec = jnp.exp(jnp.minimum(bi - bi[s:s + 1], 0.0))
            col = jnp.sum(qi * ki[s:s + 1] * dec, axis=-1, keepdims=True)
            o = o + jnp.where(rows >= s, col, 0.0) * vi[s:s + 1]
        outs.append(o)
    o = outs[0] if len(outs) == 1 else jnp.concatenate(outs, axis=0)
    b_last = bcum[c - 1:c]
    st_new = st * jnp.exp(b_last) + _mm(v, k * jnp.exp(b_last - bcum), TN)
    on = o * lax.rsqrt(jnp.mean(o * o, axis=-1, keepdims=True) + NORM_EPS) * nw
    return on * _silu(zg), st_new


def _hgrn_kernel(c, hg_ref, lb_ref, nw_ref, s0_ref, o_ref, sfin_ref):
    t = pl.program_id(1)

    @pl.when(t == 0)
    def _():
        sfin_ref[...] = s0_ref[...]

    tri = _tri_incl_bf16(c)
    qk = HG_HEADS * HG_DK
    for ci in range(hg_ref.shape[0] // c):
        r0 = ci * c
        for h in range(HG_HEADS):
            zq = hg_ref[r0:r0 + c, h * HG_DK:(h + 1) * HG_DK]
            zf = hg_ref[r0:r0 + c, qk + h * HG_DK:qk + (h + 1) * HG_DK]
            zi = hg_ref[r0:r0 + c, 2 * qk + h * HG_DV:2 * qk + (h + 1) * HG_DV]
            zg = hg_ref[r0:r0 + c, 2 * qk + HG_WIDTH + h * HG_DV:2 * qk + HG_WIDTH + (h + 1) * HG_DV]
            sl = slice(h * HG_DK, (h + 1) * HG_DK)
            out, st_new = _hgrn_chunk(zq, zf, zi, zg, sfin_ref[h], lb_ref[0:1, sl], lb_ref[1:2, sl],
                                      nw_ref[0:1, h * HG_DV:(h + 1) * HG_DV], tri)
            sfin_ref[h] = st_new
            o_ref[r0:r0 + c, h * HG_DV:(h + 1) * HG_DV] = out


def _hgrn(hgraw, lbvec, nw, s0t, ch, c):
    b, l, _ = hgraw.shape
    return pl.pallas_call(
        functools.partial(_hgrn_kernel, c),
        grid=(b, l // ch),
        in_specs=[
            pl.BlockSpec((None, ch, HG_PROJ), lambda i, j: (i, j, 0)),
            pl.BlockSpec(lbvec.shape, lambda i, j: (0, 0)),
            pl.BlockSpec(nw.shape, lambda i, j: (0, 0)),
            pl.BlockSpec((None, HG_HEADS, HG_DV, HG_DK), lambda i, j: (i, 0, 0, 0)),
        ],
        out_specs=[
            pl.BlockSpec((None, ch, HG_WIDTH), lambda i, j: (i, j, 0)),
            pl.BlockSpec((None, HG_HEADS, HG_DV, HG_DK), lambda i, j: (i, 0, 0, 0)),
        ],
        out_shape=[
            jax.ShapeDtypeStruct((b, l, HG_WIDTH), F32),
            jax.ShapeDtypeStruct((b, HG_HEADS, HG_DV, HG_DK), F32),
        ],
        compiler_params=_params(("parallel", "arbitrary")),
        name="hgrn",
    )(hgraw, lbvec, nw, s0t)


RW_PASSES_A = 1
RW_PASSES_B = 1
RW_GROUP = 2


def _rwkv_chunk_matrices(c, chunks, tri, strict, incl, same_sub):
    pa = RW_PASSES_A
    prep = []
    for r, lw, km, v, kap, beta in chunks:
        cum = _mm_exact_lhs(tri, lw)
        c_last = cum[c - 1:c]
        e_neg = jnp.exp(-cum)
        e_hat = jnp.exp(c_last - cum)
        prep.append(dict(kbar=kap * jnp.exp(cum - lw), rbar=r * jnp.exp(cum), bneg=beta * e_neg,
                         kneg=km * e_neg, bhat=beta * e_hat, khat=km * e_hat, v=v, e_last=jnp.exp(c_last)))
    e_lasts = [pr['e_last'] for pr in prep]
    streams = [(ci, h) for ci in range(len(chunks)) for h in range(RW_HEADS)]
    hs = range(len(streams))

    def part(name):
        return [prep[ci][name][:, h * RW_HEAD:(h + 1) * RW_HEAD] for ci, h in streams]

    kbar, rbar, bneg, kneg, bhat, khat, vh = (part(n) for n in ('kbar', 'rbar', 'bneg', 'kneg', 'bhat', 'khat', 'v'))
    mask_kq = jnp.concatenate([strict, incl], axis=0)
    lhs = [jnp.concatenate([kbar[h], rbar[h]], axis=0) for h in hs]
    rhs = [jnp.concatenate([bneg[h], kneg[h]], axis=0) for h in hs]
    gmat = [_mm(lhs[h], rhs[h], NT, passes=pa) for h in hs]
    aab = [jnp.where(strict, gmat[h][:c, :c], 0.0) for h in hs]
    aqb = [jnp.where(incl, gmat[h][c:, :c], 0.0) for h in hs]
    akq = [jnp.where(mask_kq, gmat[h][:, c:], 0.0) for h in hs]
    av = [_mm(akq[h], vh[h], passes=pa) for h in hs]
    d1 = [jnp.where(same_sub, aab[h], 0.0) for h in hs]
    if c == SUB:
        z = [jnp.concatenate([kbar[h], av[h][:c]], axis=1) for h in hs]
    else:
        z = [jnp.concatenate([aab[h] - d1[h], kbar[h], av[h][:c]], axis=1) for h in hs]
    d2 = [_mm(d1[h], d1[h], passes=pa) for h in hs]
    z = [z[h] - _mm(d1[h], z[h], passes=pa) for h in hs]
    d4 = [_mm(d2[h], d2[h], passes=pa) for h in hs]
    z = [z[h] + _mm(d2[h], z[h], passes=pa) for h in hs]
    d8 = [_mm(d4[h], d4[h], passes=pa) for h in hs]
    z = [z[h] + _mm(d4[h], z[h], passes=pa) for h in hs]
    z = [z[h] + _mm(d8[h], z[h], passes=pa) for h in hs]
    if c == SUB:
        x = z
    else:
        f = [z[h][:, :c] for h in hs]
        x = [z[h][:, c:] for h in hs]
        fx = [_mm(f[h], x[h], passes=pa) for h in hs]
        x = [x[h] + _mm(f[h], fx[h], passes=pa) for h in hs]
        x = [x[h] - _mm(f[h], x[h], passes=pa) for h in hs]
    qb = [_mm(aqb[h], x[h], passes=pa) for h in hs]
    rt = [rbar[h] - qb[h][:, :RW_HEAD] for h in hs]
    yc = [av[h][c:] - qb[h][:, RW_HEAD:] for h in hs]
    ktb = [_mm(x[h][:, :RW_HEAD], bhat[h], TN, passes=pa) for h in hs]
    nc = [_mm(jnp.concatenate([vh[h], x[h][:, RW_HEAD:]], axis=0),
              jnp.concatenate([khat[h], -bhat[h]], axis=0), TN, passes=pa) for h in hs]
    return rt, yc, ktb, nc, e_lasts


def _rwkv_kernel(c, rw_ref, vec_ref, s0_ref, o_ref, sfin_ref):
    t = pl.program_id(1)

    @pl.when(t == 0)
    def _():
        sfin_ref[...] = s0_ref[...]

    tri = _tri_incl_bf16(c)
    row = lax.broadcasted_iota(jnp.int32, (c, c), 0)
    col = lax.broadcasted_iota(jnp.int32, (c, c), 1)
    strict = row > col
    incl = row >= col
    same_sub = (row // SUB) == (col // SUB)
    w = RW_WIDTH
    hs = range(RW_HEADS)
    sl = [slice(h * RW_HEAD, (h + 1) * RW_HEAD) for h in hs]
    state = [sfin_ref[h] for h in hs]
    n_chunks = rw_ref.shape[0] // c
    for g0 in range(0, n_chunks, RW_GROUP):
        group = range(g0, min(g0 + RW_GROUP, n_chunks))
        chunks = [tuple(rw_ref[ci * c:(ci + 1) * c, j * w:(j + 1) * w] for j in range(6)) for ci in group]
        rt, yc, ktb, nc, e_lasts = _rwkv_chunk_matrices(c, chunks, tri, strict, incl, same_sub)
        for gi, ci in enumerate(group):
            r0 = ci * c
            r, _, km, v, _, _ = chunks[gi]
            gate = rw_ref[r0:r0 + c, 6 * w:7 * w]
            q = gi * RW_HEADS
            y = [_mm(rt[q + h], state[h], NT, passes=RW_PASSES_B) + yc[q + h] for h in hs]
            state = [state[h] * e_lasts[gi][:, sl[h]] - _mm(state[h], ktb[q + h], passes=RW_PASSES_B) + nc[q + h]
                     for h in hs]
            bonus = r * km * vec_ref[2:3, :]
            for h in hs:
                mean = jnp.mean(y[h], axis=-1, keepdims=True)
                yd = y[h] - mean
                var = jnp.mean(yd * yd, axis=-1, keepdims=True)
                yn = yd * lax.rsqrt(var + RW_GN_EPS) * vec_ref[0:1, sl[h]] + vec_ref[1:2, sl[h]]
                yn = yn + jnp.sum(bonus[:, sl[h]], axis=-1, keepdims=True) * v[:, sl[h]]
                o_ref[r0:r0 + c, sl[h]] = yn * gate[:, sl[h]]
    for h in hs:
        sfin_ref[h] = state[h]


def _rwkv(rwp, vec, s0, ch, c):
    b, l, _ = rwp.shape
    return pl.pallas_call(
        functools.partial(_rwkv_kernel, c),
        grid=(b, l // ch),
        in_specs=[
            pl.BlockSpec((None, ch, RW_PACK), lambda i, j: (i, j, 0)),
            pl.BlockSpec(vec.shape, lambda i, j: (0, 0)),
            pl.BlockSpec((None, RW_HEADS, RW_HEAD, RW_HEAD), lambda i, j: (i, 0, 0, 0)),
        ],
        out_specs=[
            pl.BlockSpec((None, ch, RW_WIDTH), lambda i, j: (i, j, 0)),
            pl.BlockSpec((None, RW_HEADS, RW_HEAD, RW_HEAD), lambda i, j: (i, 0, 0, 0)),
        ],
        out_shape=[
            jax.ShapeDtypeStruct((b, l, RW_WIDTH), F32),
            jax.ShapeDtypeStruct((b, RW_HEADS, RW_HEAD, RW_HEAD), F32),
        ],
        compiler_params=_params(("parallel", "arbitrary")),
        name="rwkv",
    )(rwp, vec, s0)


def _postmix_kernel(ohg_ref, yrw_ref, x_ref, bvec_ref, nw_ref, wout_ref, wr_ref, br_ref,
                    xn_ref, h2_ref, route_ref):
    m = (jnp.dot(ohg_ref[...].astype(BF16), wout_ref[:HG_WIDTH, :], preferred_element_type=F32)
         + jnp.dot(yrw_ref[...].astype(BF16), wout_ref[HG_WIDTH:, :], preferred_element_type=F32))
    xn = x_ref[...] + bvec_ref[0:1, :] * m
    xn_ref[...] = xn
    h2 = (xn * lax.rsqrt(jnp.mean(xn * xn, axis=-1, keepdims=True) + NORM_EPS) * nw_ref[...]
          * (1.0 + bvec_ref[2:3, :]) + bvec_ref[1:2, :])
    h2_ref[...] = h2.astype(BF16)

    logits = _mm(h2, wr_ref[...]) + br_ref[...]
    lane = lax.broadcasted_iota(jnp.int32, logits.shape, 1)
    neg = jnp.float32(-jnp.inf)
    big = jnp.int32(ROUTE_W)
    is_g = lane < N_GROUPS
    gl = jnp.where(is_g, logits, neg)
    gmax = jnp.max(gl, axis=-1, keepdims=True)
    grp = jnp.min(jnp.where(gl == gmax, lane, big), axis=-1, keepdims=True)
    p_grp = 1.0 / jnp.sum(jnp.where(is_g, jnp.exp(gl - gmax), 0.0), axis=-1, keepdims=True)
    e_lo = N_GROUPS + grp * EXPERTS_PER_GROUP
    in_grp = (lane >= e_lo) & (lane < e_lo + EXPERTS_PER_GROUP)
    el = jnp.where(in_grp, logits, neg)
    m1 = jnp.max(el, axis=-1, keepdims=True)
    i1 = jnp.min(jnp.where(el == m1, lane, big), axis=-1, keepdims=True)
    el2 = jnp.where(lane == i1, neg, el)
    m2 = jnp.max(el2, axis=-1, keepdims=True)
    i2 = jnp.min(jnp.where(el2 == m2, lane, big), axis=-1, keepdims=True)
    e21 = jnp.exp(m2 - m1)
    p1 = 1.0 / (1.0 + e21)
    p2 = e21 / (1.0 + e21)
    out = jnp.where(lane == 0, (i1 - N_GROUPS).astype(F32),
                    jnp.where(lane == 1, (i2 - N_GROUPS).astype(F32),
                              jnp.where(lane == 2, p_grp * p1,
                                        jnp.where(lane == 3, p_grp * p2, 0.0))))
    route_ref[...] = out[:, :8]


def _postmix(ohg, yrw, x, bvec, nw, wout, wr, br, tm):
    b, l, d = x.shape
    const = lambda shape: pl.BlockSpec(shape, lambda i, j: tuple(0 for _ in shape))
    return pl.pallas_call(
        _postmix_kernel,
        grid=(b, l // tm),
        in_specs=[
            pl.BlockSpec((None, tm, HG_WIDTH), lambda i, j: (i, j, 0)),
            pl.BlockSpec((None, tm, RW_WIDTH), lambda i, j: (i, j, 0)),
            pl.BlockSpec((None, tm, d), lambda i, j: (i, j, 0)),
            pl.BlockSpec((None, 8, d), lambda i, j: (i, 0, 0)),
            const(nw.shape), const(wout.shape), const(wr.shape), const(br.shape),
        ],
        out_specs=[
            pl.BlockSpec((None, tm, d), lambda i, j: (i, j, 0)),
            pl.BlockSpec((None, tm, d), lambda i, j: (i, j, 0)),
            pl.BlockSpec((None, tm, 8), lambda i, j: (i, j, 0)),
        ],
        out_shape=[
            jax.ShapeDtypeStruct((b, l, d), F32),
            jax.ShapeDtypeStruct((b, l, d), BF16),
            jax.ShapeDtypeStruct((b, l, 8), F32),
        ],
        compiler_params=_params(("parallel", "parallel")),
        name="postmix",
    )(ohg, yrw, x, bvec, nw, wout, wr, br)


def _expert_kernel(be_ref, nu_ref, xs_ref, wg_ref, wu_ref, wd_ref, o_ref):
    i = pl.program_id(0)

    @pl.when(i < nu_ref[0])
    def _():
        xb = xs_ref[...].astype(BF16)
        hg = jnp.dot(xb, wg_ref[...].astype(BF16), preferred_element_type=F32)
        hu = jnp.dot(xb, wu_ref[...].astype(BF16), preferred_element_type=F32)
        hdn = (_silu(hg) * hu).astype(BF16)
        o_ref[...] = jnp.dot(hdn, wd_ref[...].astype(BF16), preferred_element_type=F32)

    @pl.when(i >= nu_ref[0])
    def _():
        o_ref[...] = jnp.zeros_like(o_ref)


def _experts(xs, block_e, n_used, wg, wu, wd, layer, blk):
    p, d = xs.shape
    nb = p // blk
    grid_spec = pltpu.PrefetchScalarGridSpec(
        num_scalar_prefetch=2,
        grid=(nb,),
        in_specs=[
            pl.BlockSpec((blk, d), lambda i, be, nu: (i, 0)),
            pl.BlockSpec((None, None, d, D_EXPERT), lambda i, be, nu: (layer, be[i], 0, 0)),
            pl.BlockSpec((None, None, d, D_EXPERT), lambda i, be, nu: (layer, be[i], 0, 0)),
            pl.BlockSpec((None, None, D_EXPERT, d), lambda i, be, nu: (layer, be[i], 0, 0)),
        ],
        out_specs=pl.BlockSpec((blk, d), lambda i, be, nu: (i, 0)),
    )
    return pl.pallas_call(
        _expert_kernel,
        grid_spec=grid_spec,
        out_shape=jax.ShapeDtypeStruct((p, d), F32),
        compiler_params=_params(("arbitrary",)),
        name="experts",
    )(block_e, n_used, xs, wg, wu, wd)


def _combine_kernel(final, xn_ref, y0_ref, y1_ref, route_ref, bvec_ref, fw_ref, o_ref):
    rt = route_ref[...]
    moe = rt[:, 2:3] * y0_ref[...] + rt[:, 3:4] * y1_ref[...]
    x = xn_ref[...] + bvec_ref[3:4, :] * moe
    if final:
        x = x * lax.rsqrt(jnp.mean(x * x, axis=-1, keepdims=True) + NORM_EPS) * fw_ref[...]
    o_ref[...] = x


def _combine(xn, y0, y1, route, bvec, fw, final, tm):
    b, l, d = xn.shape
    row = lambda w: pl.BlockSpec((None, tm, w), lambda i, j: (i, j, 0))
    return pl.pallas_call(
        functools.partial(_combine_kernel, final),
        grid=(b, l // tm),
        in_specs=[row(d), row(d), row(d), row(8),
                  pl.BlockSpec((None, 8, d), lambda i, j: (i, 0, 0)),
                  pl.BlockSpec((1, d), lambda i, j: (0, 0))],
        out_specs=row(d),
        out_shape=jax.ShapeDtypeStruct((b, l, d), F32),
        compiler_params=_params(("parallel", "parallel")),
        name="combine",
    )(xn, y0, y1, route, bvec, fw)


def _dispatch(route, blk):
    t = route.shape[0]
    n = t * TOP_K
    flat_e = route[:, :TOP_K].astype(jnp.int32).reshape(n)
    experts = jnp.arange(N_EXPERTS, dtype=jnp.int32)
    onehot = (flat_e[:, None] == experts[None, :]).astype(jnp.int32)
    csum = jnp.cumsum(onehot, axis=0)
    counts = csum[-1]
    padded = ((counts + blk - 1) // blk) * blk
    start = jnp.cumsum(counts) - counts
    pend = jnp.cumsum(padded)
    pstart = pend - padded
    slot = jnp.sum(onehot * (csum - 1 + pstart[None, :]), axis=1)
    nb = -(-(n + N_EXPERTS * (blk - 1)) // blk)
    block_e = jnp.minimum(jnp.sum((pend // blk)[None, :] <= jnp.arange(nb, dtype=jnp.int32)[:, None], axis=1),
                          N_EXPERTS - 1).astype(jnp.int32)
    n_used = (pend[-1] // blk).astype(jnp.int32).reshape(1)
    order = jnp.argsort(flat_e).astype(jnp.int32)
    row_e = jnp.repeat(block_e, blk)
    row = jnp.arange(nb * blk, dtype=jnp.int32)
    off = row - pstart[row_e]
    src = jnp.clip(start[row_e] + off, 0, n - 1)
    buf_tok = jnp.where(off < counts[row_e], order[src] // TOP_K, 0)
    return buf_tok, slot.reshape(t, TOP_K), block_e, n_used


def _layer_weights(p, l):
    f = lambda name: p[name][l]
    zero_d = jnp.zeros((D_MODEL,), F32)
    has_vres = l > 0
    vecd = jnp.stack([f('norm_mix_w'), p['rw_mu_lora'][l, 0], p['rw_mu_lora'][l, 1], p['rw_mu_lora'][l, 2],
                      p['rw_mu_vres'][l - 1] if has_vres else zero_d, zero_d, zero_d, zero_d])
    zero_w = jnp.zeros((RW_WIDTH,), F32)
    vec5 = jnp.stack([f('rw_w0'), f('rw_a0'), p['rw_v0'][l - 1] if has_vres else zero_w,
                      f('rw_kk'), f('rw_ka'), zero_w, zero_w, zero_w])
    loras = [f('rw_w1'), f('rw_a1'), f('rw_g1')] + ([p['rw_v1'][l - 1]] if has_vres else [])
    loras += [f('rw_w2'), f('rw_a2'), f('rw_g2')] + ([p['rw_v2'][l - 1]] if has_vres else [])
    loras = [w.astype(BF16) for w in loras]
    rwvec = jnp.stack([f('rw_ln_w'), f('rw_ln_b'), f('rw_rk').reshape(RW_WIDTH),
                       zero_w, zero_w, zero_w, zero_w, zero_w])
    wr = jnp.concatenate([f('router_g_w'),
                          jnp.transpose(f('router_e_w'), (1, 0, 2)).reshape(D_MODEL, N_EXPERTS),
                          jnp.zeros((D_MODEL, ROUTE_W - N_GROUPS - N_EXPERTS), F32)], axis=1)
    br = jnp.concatenate([f('router_g_b'), f('router_e_b').reshape(N_EXPERTS),
                          jnp.zeros((ROUTE_W - N_GROUPS - N_EXPERTS,), F32)]).reshape(1, ROUTE_W)
    return dict(
        vecd=vecd, vec5=vec5, mup=f('rw_mu_proj').reshape(1, RW_PROJ), win=f('w_in').astype(BF16),
        loras=loras, rwvec=rwvec, hg_nw=f('hg_norm_w').reshape(1, HG_WIDTH),
        nfw=f('norm_ffn_w').reshape(1, D_MODEL), wout=f('w_out').astype(BF16), wr=wr, br=br)


def _trunk(x, mod, s_hg, s_rw, h_prev, lw_all, expert_w, lbvecs, final_w):
    b, l, d = x.shape
    depth = len(lw_all)
    tm = min(l, 256)
    ch = min(l, 256)
    c = min(l, CHUNK)
    blk = 256 if b * l * TOP_K >= 8192 else 32
    head = lax.broadcasted_iota(jnp.int32, (RW_WIDTH, RW_WIDTH), 0) // RW_HEAD
    seg = (head == head.T).astype(BF16)
    new_hg, new_rw, new_shift = [], [], []
    rwp_first = None
    zrow = jnp.zeros((b, d), F32)
    for li in range(depth):
        w = lw_all[li]
        sh1, sc1, g1, sh2, sc2, g2 = jnp.split(mod[li], 6, axis=-1)
        bvec1 = jnp.stack([sh1, sc1, h_prev[li], zrow, zrow, zrow, zrow, zrow], axis=1)
        bvec2 = jnp.stack([g1, sh2, sc2, g2, zrow, zrow, zrow, zrow], axis=1)
        hgraw, rwp, hlast = _premix(x, bvec1, w['vecd'], w['vec5'], w['mup'], w['win'], w['loras'],
                                    seg, rwp_first if li > 0 else None, tm)
        if li == 0:
            rwp_first = rwp
        ohg, hg_t = _hgrn(hgraw, lbvecs[li], w['hg_nw'], jnp.swapaxes(s_hg[li], -1, -2), ch, c)
        yrw, rw_fin = _rwkv(rwp, w['rwvec'], s_rw[li], ch, c)
        xn, h2, route = _postmix(ohg, yrw, x, bvec2, w['nfw'], w['wout'], w['wr'], w['br'], tm)
        t = b * l
        buf_tok, slot, block_e, n_used = _dispatch(route.reshape(t, 8), blk)
        xs = h2.reshape(t, d).at[buf_tok].get(mode='promise_in_bounds')
        yb = _experts(xs, block_e, n_used, *expert_w, li, blk)
        y0 = yb.at[slot[:, 0]].get(mode='promise_in_bounds').reshape(b, l, d)
        y1 = yb.at[slot[:, 1]].get(mode='promise_in_bounds').reshape(b, l, d)
        x = _combine(xn, y0, y1, route, bvec2, final_w, li == depth - 1, tm)
        new_hg.append(jnp.swapaxes(hg_t, -1, -2))
        new_rw.append(rw_fin)
        new_shift.append(hlast.reshape(b, d))
    return x, jnp.stack(new_hg), jnp.stack(new_rw), jnp.stack(new_shift)


def kernel(x_prompt, x_sample, state_hgrn, state_rwkv, state_shift, c_prompt, c_sample, w_ada, b_ada, norm_mix_w, norm_ffn_w, w_in, w_out, hg_lb_logits, hg_norm_w, rw_mu_proj, rw_mu_lora, rw_w0, rw_w1, rw_w2, rw_a0, rw_a1, rw_a2, rw_g1, rw_g2, rw_mu_vres, rw_v0, rw_v1, rw_v2, rw_kk, rw_ka, rw_rk, rw_ln_w, rw_ln_b, router_g_w, router_g_b, router_e_w, router_e_b, w_gate, w_up, w_down, final_norm_w):
    p = dict(norm_mix_w=norm_mix_w, norm_ffn_w=norm_ffn_w, w_in=w_in, w_out=w_out, hg_norm_w=hg_norm_w,
             rw_mu_proj=rw_mu_proj, rw_mu_lora=rw_mu_lora, rw_w0=rw_w0, rw_w1=rw_w1, rw_w2=rw_w2,
             rw_a0=rw_a0, rw_a1=rw_a1, rw_a2=rw_a2, rw_g1=rw_g1, rw_g2=rw_g2,
             rw_mu_vres=rw_mu_vres, rw_v0=rw_v0, rw_v1=rw_v1, rw_v2=rw_v2,
             rw_kk=rw_kk, rw_ka=rw_ka, rw_rk=rw_rk, rw_ln_w=rw_ln_w, rw_ln_b=rw_ln_b,
             router_g_w=router_g_w, router_g_b=router_g_b, router_e_w=router_e_w,
             router_e_b=router_e_b, w_gate=w_gate, w_up=w_up, w_down=w_down)
    depth = w_in.shape[0]
    bp = x_prompt.shape[0]
    bs = x_sample.shape[0]
    d = x_prompt.shape[-1]
    lw_all = [_layer_weights(p, l) for l in range(depth)]
    lb = jnp.cumsum(jax.nn.softmax(hg_lb_logits.astype(F32), axis=0), axis=0)
    lb = lb - lb[0]
    zpad = jnp.zeros((6, lb.shape[1]), F32)
    lbvecs = [jnp.concatenate([jnp.log(lb[l])[None], jnp.log1p(-lb[l])[None], zpad], axis=0)
              for l in range(depth)]
    rows = -(-(bp + bs) // 8) * 8
    c_all = jnp.concatenate([c_prompt, c_sample, jnp.zeros((rows - bp - bs, d), F32)], axis=0)
    mod = _ada_mod(c_all, w_ada, b_ada)
    fw = final_norm_w.reshape(1, d)
    zero_hg = jnp.zeros((depth, bp) + state_hgrn.shape[2:], F32)
    zero_rw = jnp.zeros((depth, bp) + state_rwkv.shape[2:], F32)
    zero_sh = jnp.zeros((depth, bp, d), F32)
    expert_w = (w_gate, w_up, w_down)
    y_p, hg_p, rw_p, sh_p = _trunk(x_prompt, mod[:, :bp], zero_hg, zero_rw, zero_sh, lw_all, expert_w,
                                   lbvecs, fw)
    y_s, hg_s, rw_s, sh_s = _trunk(x_sample, mod[:, bp:bp + bs], state_hgrn, state_rwkv, state_shift,
                                   lw_all, expert_w, lbvecs, fw)
    return (y_p, y_s, hg_p, rw_p, sh_p, hg_s, rw_s, sh_s)
```

```python
import functools

import jax
import jax.numpy as jnp
from jax import lax
from jax.experimental import pallas as pl
from jax.experimental.pallas import tpu as pltpu

F32 = jnp.float32
BF16 = jnp.bfloat16

D_MODEL = 1024
HG_HEADS = 4
HG_DK = 128
HG_DV = 128
HG_WIDTH = HG_HEADS * HG_DV
RW_HEADS = 8
RW_HEAD = 64
RW_WIDTH = RW_HEADS * RW_HEAD
HG_PROJ = 2 * HG_HEADS * HG_DK + 2 * HG_WIDTH
RW_PROJ = 3 * RW_WIDTH
IN_WIDTH = HG_PROJ + RW_PROJ
N_GROUPS = 4
EXPERTS_PER_GROUP = 8
N_EXPERTS = N_GROUPS * EXPERTS_PER_GROUP
TOP_K = 2
D_EXPERT = 512
LOG2E = 1.4426950408889634
NORM_EPS = 1e-6
RW_GN_EPS = 64e-5

CHUNK = 64
SUB = 16
HG_SUB = 8
ROUTE_W = 128
RW_PACK = 7 * RW_WIDTH
VMEM_LIMIT = 56 * 1024 * 1024

NN = (((1,), (0,)), ((), ()))
NT = (((1,), (1,)), ((), ()))
TN = (((0,), (0,)), ((), ()))


def _split2(a):
    hi = a.astype(BF16)
    lo = (a - hi.astype(F32)).astype(BF16)
    return hi, lo


def _mm(a, b, dims=NN, passes=1):
    if passes == 1:
        return lax.dot_general(a.astype(BF16), b.astype(BF16), dims, preferred_element_type=F32)
    a1, a2 = _split2(a)
    b1, b2 = _split2(b)
    out = lax.dot_general(a1, b1, dims, preferred_element_type=F32)
    out = out + lax.dot_general(a1, b2, dims, preferred_element_type=F32)
    return out + lax.dot_general(a2, b1, dims, preferred_element_type=F32)


def _mm_exact_lhs(a_bf, b):
    b1 = b.astype(BF16)
    r = b - b1.astype(F32)
    b2 = r.astype(BF16)
    b3 = (r - b2.astype(F32)).astype(BF16)
    out = jnp.dot(a_bf, b1, preferred_element_type=F32)
    out = out + jnp.dot(a_bf, b2, preferred_element_type=F32)
    return out + jnp.dot(a_bf, b3, preferred_element_type=F32)


def _sigmoid(x):
    return 1.0 / (1.0 + jnp.exp(-x))


def _silu(x):
    return x * _sigmoid(x)


def _softplus(x):
    return jnp.maximum(x, 0.0) + jnp.log1p(jnp.exp(-jnp.abs(x)))


def _tri_incl_bf16(n):
    r = lax.broadcasted_iota(jnp.int32, (n, n), 0)
    c = lax.broadcasted_iota(jnp.int32, (n, n), 1)
    return (r >= c).astype(BF16)


def _params(sem):
    return pltpu.CompilerParams(dimension_semantics=sem, vmem_limit_bytes=VMEM_LIMIT)


def _ada_kernel(c_ref, w_ref, b_ref, o_ref):
    cs = _silu(c_ref[...])
    o_ref[...] = _mm(cs, w_ref[...]) + b_ref[...]


def _ada_mod(c_all, w_ada, b_ada):
    depth, d, n = w_ada.shape
    rows = c_all.shape[0]
    tn = 1536
    return pl.pallas_call(
        _ada_kernel,
        grid=(depth, n // tn),
        in_specs=[
            pl.BlockSpec((rows, d), lambda l, j: (0, 0)),
            pl.BlockSpec((None, d, tn), lambda l, j: (l, 0, j)),
            pl.BlockSpec((None, 1, tn), lambda l, j: (l, 0, j)),
        ],
        out_specs=pl.BlockSpec((None, rows, tn), lambda l, j: (l, 0, j)),
        out_shape=jax.ShapeDtypeStruct((depth, rows, n), F32),
        compiler_params=_params(("parallel", "parallel")),
        name="ada_mod",
    )(c_all, w_ada, b_ada.reshape(depth, 1, n))


def _premix_kernel(has_vres, *refs):
    if has_vres:
        (x_ref, bvec_ref, vecd_ref, vec5_ref, mup_ref, win_ref, w1_ref, a1_ref, g1_ref, v1_ref,
         w2_ref, a2_ref, g2_ref, v2_ref, seg_ref, vfirst_ref,
         hg_ref, rw_ref, hlast_ref, hcar, pcar) = refs
    else:
        (x_ref, bvec_ref, vecd_ref, vec5_ref, mup_ref, win_ref, w1_ref, a1_ref, g1_ref,
         w2_ref, a2_ref, g2_ref, seg_ref,
         hg_ref, rw_ref, hlast_ref, hcar, pcar) = refs
    t = pl.program_id(1)
    tm = x_ref.shape[0]
    w_rw = win_ref[:, HG_PROJ:]

    @pl.when(t == 0)
    def _():
        hp = jnp.broadcast_to(bvec_ref[2:3, :], (8, D_MODEL))
        hcar[...] = hp
        pcar[...] = _mm(hp, w_rw)

    x = x_ref[...]
    nw = vecd_ref[0:1, :]
    sh = bvec_ref[0:1, :]
    sc = bvec_ref[1:2, :]
    xn = x * lax.rsqrt(jnp.mean(x * x, axis=-1, keepdims=True) + NORM_EPS) * nw
    h = xn * (1.0 + sc) + sh

    hb = h.astype(BF16)
    hg_ref[...] = jnp.dot(hb, win_ref[:, :HG_PROJ], preferred_element_type=F32)
    p_cur = jnp.dot(hb, w_rw, preferred_element_type=F32)

    first = lax.broadcasted_iota(jnp.int32, (tm, 1), 0) == 0
    h_shift = jnp.where(first, hcar[0:1, :], pltpu.roll(h, 1, 0))
    p_shift = jnp.where(first, pcar[0:1, :], pltpu.roll(p_cur, 1, 0))
    hcar[0:1, :] = h[tm - 1:tm, :]
    pcar[0:1, :] = p_cur[tm - 1:tm, :]
    hlast_ref[...] = h[tm - 1:tm, :]

    pm = p_cur + (p_shift - p_cur) * mup_ref[...]
    r = pm[:, :RW_WIDTH]
    k = pm[:, RW_WIDTH:2 * RW_WIDTH]
    v = pm[:, 2 * RW_WIDTH:]

    dh = h_shift - h
    xw = h + dh * vecd_ref[1:2, :]
    xa = h + dh * vecd_ref[2:3, :]
    xg = h + dh * vecd_ref[3:4, :]
    w_pre = vec5_ref[0:1, :] + _mm(jnp.tanh(_mm(xw, w1_ref[...])), w2_ref[...])
    logw = -jnp.exp(-_softplus(-w_pre) - 0.5)
    a = _sigmoid(vec5_ref[1:2, :] + _mm(_mm(xa, a1_ref[...]), a2_ref[...]))
    g = _mm(_sigmoid(_mm(xg, g1_ref[...])), g2_ref[...])
    if has_vres:
        xv = h + dh * vecd_ref[4:5, :]
        nu = _sigmoid(vec5_ref[2:3, :] + _mm(_mm(xv, v1_ref[...]), v2_ref[...]))
        v = v + (vfirst_ref[...] - v) * nu

    kk = k * vec5_ref[3:4, :]
    ss = _mm(kk * kk, seg_ref[...], passes=3)
    kappa = kk / jnp.maximum(jnp.sqrt(ss), 1e-12)
    k_mod = k * (1.0 + (a - 1.0) * vec5_ref[4:5, :])

    rw_ref[:, 0 * RW_WIDTH:1 * RW_WIDTH] = r
    rw_ref[:, 1 * RW_WIDTH:2 * RW_WIDTH] = logw
    rw_ref[:, 2 * RW_WIDTH:3 * RW_WIDTH] = k_mod
    rw_ref[:, 3 * RW_WIDTH:4 * RW_WIDTH] = v
    rw_ref[:, 4 * RW_WIDTH:5 * RW_WIDTH] = kappa
    rw_ref[:, 5 * RW_WIDTH:6 * RW_WIDTH] = kappa * a
    rw_ref[:, 6 * RW_WIDTH:7 * RW_WIDTH] = g


def _premix(x, bvec, vecd, vec5, mup, win, loras, seg, vfirst, tm):
    b, l, d = x.shape
    has_vres = vfirst is not None
    const = lambda shape: pl.BlockSpec(shape, lambda i, j: tuple(0 for _ in shape))
    in_specs = [
        pl.BlockSpec((None, tm, d), lambda i, j: (i, j, 0)),
        pl.BlockSpec((None, 8, d), lambda i, j: (i, 0, 0)),
        const(vecd.shape), const(vec5.shape), const(mup.shape), const(win.shape),
    ]
    in_specs += [const(w.shape) for w in loras]
    in_specs.append(const(seg.shape))
    args = [x, bvec, vecd, vec5, mup, win, *loras, seg]
    if has_vres:
        in_specs.append(pl.BlockSpec((None, tm, RW_WIDTH), lambda i, j: (i, j, 3)))
        args.append(vfirst)
    return pl.pallas_call(
        functools.partial(_premix_kernel, has_vres),
        grid=(b, l // tm),
        in_specs=in_specs,
        out_specs=[
            pl.BlockSpec((None, tm, HG_PROJ), lambda i, j: (i, j, 0)),
            pl.BlockSpec((None, tm, RW_PACK), lambda i, j: (i, j, 0)),
            pl.BlockSpec((None, 1, d), lambda i, j: (i, 0, 0)),
        ],
        out_shape=[
            jax.ShapeDtypeStruct((b, l, HG_PROJ), F32),
            jax.ShapeDtypeStruct((b, l, RW_PACK), F32),
            jax.ShapeDtypeStruct((b, 1, d), F32),
        ],
        scratch_shapes=[pltpu.VMEM((8, d), F32), pltpu.VMEM((8, RW_PROJ), F32)],
        compiler_params=_params(("parallel", "arbitrary")),
        name="premix",
    )(*args)


def _hgrn_chunk(zq, zf, zi, zg, st, loglb, log1mlb, nw, tri):
    c = zq.shape[0]
    q = _silu(zq)
    bterm = log1mlb + (jnp.minimum(zf, 0.0) - jnp.log1p(jnp.exp(-jnp.abs(zf))))
    logf = jnp.maximum(loglb, bterm) + jnp.log1p(jnp.exp(-jnp.abs(loglb - bterm)))
    k = 1.0 - jnp.exp(logf)
    v = zi
    bcum = _mm_exact_lhs(tri, logf) * LOG2E
    row = lax.broadcasted_iota(jnp.int32, (c, 1), 0)
    rr = lax.broadcasted_iota(jnp.int32, (c, c), 0)
    cc = lax.broadcasted_iota(jnp.int32, (c, c), 1)
    a = jnp.zeros((c, c), F32)
    m = c // 2
    while m >= HG_SUB:
        bmid = jnp.concatenate([jnp.broadcast_to(bcum[(2 * j + 1) * m - 1:(2 * j + 1) * m], (2 * m, HG_DK))
                                for j in range(c // (2 * m))], axis=0)
        upper = ((row // m) % 2) == 1
        qt = jnp.where(upper, q * jnp.exp2(bcum - bmid), 0.0)
        kt = jnp.where(upper, 0.0, k * jnp.exp2(bmid - bcum))
        a = a + jnp.where((rr // (2 * m)) == (cc // (2 * m)), _mm(qt, kt, NT), 0.0)
        m //= 2
    lane = lax.broadcasted_iota(jnp.int32, (HG_SUB, c), 1)
    rws = lax.broadcasted_iota(jnp.int32, (HG_SUB, c), 0)
    diag = []
    for i in range(c // HG_SUB):
        lo = i * HG_SUB
        bi = bcum[lo:lo + HG_SUB]
        qi = q[lo:lo + HG_SUB]
        ki = k[lo:lo + HG_SUB]
        d = jnp.zeros((HG_SUB, c), F32)
        for s in range(HG_SUB):
            dec = jnp.exp2(bi - bi[s:s + 1])
            col = jnp.sum(qi * ki[s:s + 1] * dec, axis=-1, keepdims=True)
            d = jnp.where((lane == lo + s) & (rws >= s), col, d)
        diag.append(d)
    a = a + jnp.concatenate(diag, axis=0)
    o = _mm(q * jnp.exp2(bcum), st, NT) + _mm(a, v)
    b_last = bcum[c - 1:c]
    st_new = st * jnp.exp2(b_last) + _mm(v, k * jnp.exp2(b_last - bcum), TN)
    on = o * lax.rsqrt(jnp.mean(o * o, axis=-1, keepdims=True) + NORM_EPS) * nw
    return on * _silu(zg), st_new


def _hgrn_kernel(c, hg_ref, lb_ref, nw_ref, s0_ref, o_ref, sfin_ref):
    t = pl.program_id(1)

    @pl.when(t == 0)
    def _():
        sfin_ref[...] = s0_ref[...]

    tri = _tri_incl_bf16(c)
    qk = HG_HEADS * HG_DK
    for ci in range(hg_ref.shape[0] // c):
        r0 = ci * c
        for h in range(HG_HEADS):
            zq = hg_ref[r0:r0 + c, h * HG_DK:(h + 1) * HG_DK]
            zf = hg_ref[r0:r0 + c, qk + h * HG_DK:qk + (h + 1) * HG_DK]
            zi = hg_ref[r0:r0 + c, 2 * qk + h * HG_DV:2 * qk + (h + 1) * HG_DV]
            zg = hg_ref[r0:r0 + c, 2 * qk + HG_WIDTH + h * HG_DV:2 * qk + HG_WIDTH + (h + 1) * HG_DV]
            sl = slice(h * HG_DK, (h + 1) * HG_DK)
            out, st_new = _hgrn_chunk(zq, zf, zi, zg, sfin_ref[h], lb_ref[0:1, sl], lb_ref[1:2, sl],
                                      nw_ref[0:1, h * HG_DV:(h + 1) * HG_DV], tri)
            sfin_ref[h] = st_new
            o_ref[r0:r0 + c, h * HG_DV:(h + 1) * HG_DV] = out


def _hgrn(hgraw, lbvec, nw, s0t, ch, c):
    b, l, _ = hgraw.shape
    return pl.pallas_call(
        functools.partial(_hgrn_kernel, c),
        grid=(b, l // ch),
        in_specs=[
            pl.BlockSpec((None, ch, HG_PROJ), lambda i, j: (i, j, 0)),
            pl.BlockSpec(lbvec.shape, lambda i, j: (0, 0)),
            pl.BlockSpec(nw.shape, lambda i, j: (0, 0)),
            pl.BlockSpec((None, HG_HEADS, HG_DV, HG_DK), lambda i, j: (i, 0, 0, 0)),
        ],
        out_specs=[
            pl.BlockSpec((None, ch, HG_WIDTH), lambda i, j: (i, j, 0)),
            pl.BlockSpec((None, HG_HEADS, HG_DV, HG_DK), lambda i, j: (i, 0, 0, 0)),
        ],
        out_shape=[
            jax.ShapeDtypeStruct((b, l, HG_WIDTH), F32),
            jax.ShapeDtypeStruct((b, HG_HEADS, HG_DV, HG_DK), F32),
        ],
        compiler_params=_params(("parallel", "arbitrary")),
        name="hgrn",
    )(hgraw, lbvec, nw, s0t)


RW_PASSES_A = 1
RW_PASSES_B = 1
RW_GROUP = 2


def _rwkv_chunk_matrices(c, chunks, tri, strict, incl, same_sub):
    pa = RW_PASSES_A
    prep = []
    for r, lw, km, v, kap, beta in chunks:
        cum = _mm_exact_lhs(tri, lw)
        c_last = cum[c - 1:c]
        e_neg = jnp.exp(-cum)
        e_hat = jnp.exp(c_last - cum)
        prep.append(dict(kbar=kap * jnp.exp(cum - lw), rbar=r * jnp.exp(cum), bneg=beta * e_neg,
                         kneg=km * e_neg, bhat=beta * e_hat, khat=km * e_hat, v=v, e_last=jnp.exp(c_last)))
    e_lasts = [pr['e_last'] for pr in prep]
    streams = [(ci, h) for ci in range(len(chunks)) for h in range(RW_HEADS)]
    hs = range(len(streams))

    def part(name):
        return [prep[ci][name][:, h * RW_HEAD:(h + 1) * RW_HEAD] for ci, h in streams]

    kbar, rbar, bneg, kneg, bhat, khat, vh = (part(n) for n in ('kbar', 'rbar', 'bneg', 'kneg', 'bhat', 'khat', 'v'))
    mask_kq = jnp.concatenate([strict, incl], axis=0)
    lhs = [jnp.concatenate([kbar[h], rbar[h]], axis=0) for h in hs]
    rhs = [jnp.concatenate([bneg[h], kneg[h]], axis=0) for h in hs]
    gmat = [_mm(lhs[h], rhs[h], NT, passes=pa) for h in hs]
    aab = [jnp.where(strict, gmat[h][:c, :c], 0.0) for h in hs]
    aqb = [jnp.where(incl, gmat[h][c:, :c], 0.0) for h in hs]
    akq = [jnp.where(mask_kq, gmat[h][:, c:], 0.0) for h in hs]
    av = [_mm(akq[h], vh[h], passes=pa) for h in hs]
    d1 = [jnp.where(same_sub, aab[h], 0.0) for h in hs]
    if c == SUB:
        z = [jnp.concatenate([kbar[h], av[h][:c]], axis=1) for h in hs]
    else:
        z = [jnp.concatenate([aab[h] - d1[h], kbar[h], av[h][:c]], axis=1) for h in hs]
    d2 = [_mm(d1[h], d1[h], passes=pa) for h in hs]
    z = [z[h] - _mm(d1[h], z[h], passes=pa) for h in hs]
    d4 = [_mm(d2[h], d2[h], passes=pa) for h in hs]
    z = [z[h] + _mm(d2[h], z[h], passes=pa) for h in hs]
    d8 = [_mm(d4[h], d4[h], passes=pa) for h in hs]
    z = [z[h] + _mm(d4[h], z[h], passes=pa) for h in hs]
    z = [z[h] + _mm(d8[h], z[h], passes=pa) for h in hs]
    if c == SUB:
        x = z
    else:
        f = [z[h][:, :c] for h in hs]
        x = [z[h][:, c:] for h in hs]
        fx = [_mm(f[h], x[h], passes=pa) for h in hs]
        x = [x[h] + _mm(f[h], fx[h], passes=pa) for h in hs]
        x = [x[h] - _mm(f[h], x[h], passes=pa) for h in hs]
    qb = [_mm(aqb[h], x[h], passes=pa) for h in hs]
    rt = [rbar[h] - qb[h][:, :RW_HEAD] for h in hs]
    yc = [av[h][c:] - qb[h][:, RW_HEAD:] for h in hs]
    ktb = [_mm(x[h][:, :RW_HEAD], bhat[h], TN, passes=pa) for h in hs]
    nc = [_mm(jnp.concatenate([vh[h], x[h][:, RW_HEAD:]], axis=0),
              jnp.concatenate([khat[h], -bhat[h]], axis=0), TN, passes=pa) for h in hs]
    return rt, yc, ktb, nc, e_lasts


def _rwkv_kernel(c, rw_ref, vec_ref, s0_ref, o_ref, sfin_ref):
    t = pl.program_id(1)

    @pl.when(t == 0)
    def _():
        sfin_ref[...] = s0_ref[...]

    tri = _tri_incl_bf16(c)
    row = lax.broadcasted_iota(jnp.int32, (c, c), 0)
    col = lax.broadcasted_iota(jnp.int32, (c, c), 1)
    strict = row > col
    incl = row >= col
    same_sub = (row // SUB) == (col // SUB)
    w = RW_WIDTH
    hs = range(RW_HEADS)
    sl = [slice(h * RW_HEAD, (h + 1) * RW_HEAD) for h in hs]
    state = [sfin_ref[h] for h in hs]
    n_chunks = rw_ref.shape[0] // c
    for g0 in range(0, n_chunks, RW_GROUP):
        group = range(g0, min(g0 + RW_GROUP, n_chunks))
        chunks = [tuple(rw_ref[ci * c:(ci + 1) * c, j * w:(j + 1) * w] for j in range(6)) for ci in group]
        rt, yc, ktb, nc, e_lasts = _rwkv_chunk_matrices(c, chunks, tri, strict, incl, same_sub)
        for gi, ci in enumerate(group):
            r0 = ci * c
            r, _, km, v, _, _ = chunks[gi]
            gate = rw_ref[r0:r0 + c, 6 * w:7 * w]
            q = gi * RW_HEADS
            y = [_mm(rt[q + h], state[h], NT, passes=RW_PASSES_B) + yc[q + h] for h in hs]
            state = [state[h] * e_lasts[gi][:, sl[h]] - _mm(state[h], ktb[q + h], passes=RW_PASSES_B) + nc[q + h]
                     for h in hs]
            bonus = r * km * vec_ref[2:3, :]
            for h in hs:
                mean = jnp.mean(y[h], axis=-1, keepdims=True)
                yd = y[h] - mean
                var = jnp.mean(yd * yd, axis=-1, keepdims=True)
                yn = yd * lax.rsqrt(var + RW_GN_EPS) * vec_ref[0:1, sl[h]] + vec_ref[1:2, sl[h]]
                yn = yn + jnp.sum(bonus[:, sl[h]], axis=-1, keepdims=True) * v[:, sl[h]]
                o_ref[r0:r0 + c, sl[h]] = yn * gate[:, sl[h]]
    for h in hs:
        sfin_ref[h] = state[h]


def _rwkv(rwp, vec, s0, ch, c):
    b, l, _ = rwp.shape
    return pl.pallas_call(
        functools.partial(_rwkv_kernel, c),
        grid=(b, l // ch),
        in_specs=[
            pl.BlockSpec((None, ch, RW_PACK), lambda i, j: (i, j, 0)),
            pl.BlockSpec(vec.shape, lambda i, j: (0, 0)),
            pl.BlockSpec((None, RW_HEADS, RW_HEAD, RW_HEAD), lambda i, j: (i, 0, 0, 0)),
        ],
        out_specs=[
            pl.BlockSpec((None, ch, RW_WIDTH), lambda i, j: (i, j, 0)),
            pl.BlockSpec((None, RW_HEADS, RW_HEAD, RW_HEAD), lambda i, j: (i, 0, 0, 0)),
        ],
        out_shape=[
            jax.ShapeDtypeStruct((b, l, RW_WIDTH), F32),
            jax.ShapeDtypeStruct((b, RW_HEADS, RW_HEAD, RW_HEAD), F32),
        ],
        compiler_params=_params(("parallel", "arbitrary")),
        name="rwkv",
    )(rwp, vec, s0)


def _postmix_kernel(ohg_ref, yrw_ref, x_ref, bvec_ref, nw_ref, wout_ref, wr_ref, br_ref,
                    xn_ref, h2_ref, route_ref):
    m = (jnp.dot(ohg_ref[...].astype(BF16), wout_ref[:HG_WIDTH, :], preferred_element_type=F32)
         + jnp.dot(yrw_ref[...].astype(BF16), wout_ref[HG_WIDTH:, :], preferred_element_type=F32))
    xn = x_ref[...] + bvec_ref[0:1, :] * m
    xn_ref[...] = xn
    h2 = (xn * lax.rsqrt(jnp.mean(xn * xn, axis=-1, keepdims=True) + NORM_EPS) * nw_ref[...]
          * (1.0 + bvec_ref[2:3, :]) + bvec_ref[1:2, :])
    h2_ref[...] = h2.astype(BF16)

    logits = _mm(h2, wr_ref[...]) + br_ref[...]
    lane = lax.broadcasted_iota(jnp.int32, logits.shape, 1)
    neg = jnp.float32(-jnp.inf)
    big = jnp.int32(ROUTE_W)
    is_g = lane < N_GROUPS
    gl = jnp.where(is_g, logits, neg)
    gmax = jnp.max(gl, axis=-1, keepdims=True)
    grp = jnp.min(jnp.where(gl == gmax, lane, big), axis=-1, keepdims=True)
    p_grp = 1.0 / jnp.sum(jnp.where(is_g, jnp.exp(gl - gmax), 0.0), axis=-1, keepdims=True)
    e_lo = N_GROUPS + grp * EXPERTS_PER_GROUP
    in_grp = (lane >= e_lo) & (lane < e_lo + EXPERTS_PER_GROUP)
    el = jnp.where(in_grp, logits, neg)
    m1 = jnp.max(el, axis=-1, keepdims=True)
    i1 = jnp.min(jnp.where(el == m1, lane, big), axis=-1, keepdims=True)
    el2 = jnp.where(lane == i1, neg, el)
    m2 = jnp.max(el2, axis=-1, keepdims=True)
    i2 = jnp.min(jnp.where(el2 == m2, lane, big), axis=-1, keepdims=True)
    e21 = jnp.exp(m2 - m1)
    p1 = 1.0 / (1.0 + e21)
    p2 = e21 / (1.0 + e21)
    out = jnp.where(lane == 0, (i1 - N_GROUPS).astype(F32),
                    jnp.where(lane == 1, (i2 - N_GROUPS).astype(F32),
                              jnp.where(lane == 2, p_grp * p1,
                                        jnp.where(lane == 3, p_grp * p2, 0.0))))
    route_ref[...] = out[:, :8]


def _postmix(ohg, yrw, x, bvec, nw, wout, wr, br, tm):
    b, l, d = x.shape
    const = lambda shape: pl.BlockSpec(shape, lambda i, j: tuple(0 for _ in shape))
    return pl.pallas_call(
        _postmix_kernel,
        grid=(b, l // tm),
        in_specs=[
            pl.BlockSpec((None, tm, HG_WIDTH), lambda i, j: (i, j, 0)),
            pl.BlockSpec((None, tm, RW_WIDTH), lambda i, j: (i, j, 0)),
            pl.BlockSpec((None, tm, d), lambda i, j: (i, j, 0)),
            pl.BlockSpec((None, 8, d), lambda i, j: (i, 0, 0)),
            const(nw.shape), const(wout.shape), const(wr.shape), const(br.shape),
        ],
        out_specs=[
            pl.BlockSpec((None, tm, d), lambda i, j: (i, j, 0)),
            pl.BlockSpec((None, tm, d), lambda i, j: (i, j, 0)),
            pl.BlockSpec((None, tm, 8), lambda i, j: (i, j, 0)),
        ],
        out_shape=[
            jax.ShapeDtypeStruct((b, l, d), F32),
            jax.ShapeDtypeStruct((b, l, d), BF16),
            jax.ShapeDtypeStruct((b, l, 8), F32),
        ],
        compiler_params=_params(("parallel", "parallel")),
        name="postmix",
    )(ohg, yrw, x, bvec, nw, wout, wr, br)


def _expert_kernel(be_ref, nu_ref, xs_ref, wg_ref, wu_ref, wd_ref, o_ref):
    i = pl.program_id(0)

    @pl.when(i < nu_ref[0])
    def _():
        xb = xs_ref[...].astype(BF16)
        hg = jnp.dot(xb, wg_ref[...].astype(BF16), preferred_element_type=F32)
        hu = jnp.dot(xb, wu_ref[...].astype(BF16), preferred_element_type=F32)
        hdn = (_silu(hg) * hu).astype(BF16)
        o_ref[...] = jnp.dot(hdn, wd_ref[...].astype(BF16), preferred_element_type=F32)

    @pl.when(i >= nu_ref[0])
    def _():
        o_ref[...] = jnp.zeros_like(o_ref)


def _experts(xs, block_e, n_used, wg, wu, wd, layer, blk):
    p, d = xs.shape
    nb = p // blk
    grid_spec = pltpu.PrefetchScalarGridSpec(
        num_scalar_prefetch=2,
        grid=(nb,),
        in_specs=[
            pl.BlockSpec((blk, d), lambda i, be, nu: (i, 0)),
            pl.BlockSpec((None, None, d, D_EXPERT), lambda i, be, nu: (layer, be[i], 0, 0)),
            pl.BlockSpec((None, None, d, D_EXPERT), lambda i, be, nu: (layer, be[i], 0, 0)),
            pl.BlockSpec((None, None, D_EXPERT, d), lambda i, be, nu: (layer, be[i], 0, 0)),
        ],
        out_specs=pl.BlockSpec((blk, d), lambda i, be, nu: (i, 0)),
    )
    return pl.pallas_call(
        _expert_kernel,
        grid_spec=grid_spec,
        out_shape=jax.ShapeDtypeStruct((p, d), F32),
        compiler_params=_params(("arbitrary",)),
        name="experts",
    )(block_e, n_used, xs, wg, wu, wd)


def _combine_kernel(final, xn_ref, y0_ref, y1_ref, route_ref, bvec_ref, fw_ref, o_ref):
    rt = route_ref[...]
    moe = rt[:, 2:3] * y0_ref[...] + rt[:, 3:4] * y1_ref[...]
    x = xn_ref[...] + bvec_ref[3:4, :] * moe
    if final:
        x = x * lax.rsqrt(jnp.mean(x * x, axis=-1, keepdims=True) + NORM_EPS) * fw_ref[...]
    o_ref[...] = x


def _combine(xn, y0, y1, route, bvec, fw, final, tm):
    b, l, d = xn.shape
    row = lambda w: pl.BlockSpec((None, tm, w), lambda i, j: (i, j, 0))
    return pl.pallas_call(
        functools.partial(_combine_kernel, final),
        grid=(b, l // tm),
        in_specs=[row(d), row(d), row(d), row(8),
                  pl.BlockSpec((None, 8, d), lambda i, j: (i, 0, 0)),
                  pl.BlockSpec((1, d), lambda i, j: (0, 0))],
        out_specs=row(d),
        out_shape=jax.ShapeDtypeStruct((b, l, d), F32),
        compiler_params=_params(("parallel", "parallel")),
        name="combine",
    )(xn, y0, y1, route, bvec, fw)


def _dispatch(route, blk):
    t = route.shape[0]
    n = t * TOP_K
    flat_e = route[:, :TOP_K].astype(jnp.int32).reshape(n)
    experts = jnp.arange(N_EXPERTS, dtype=jnp.int32)
    onehot = (flat_e[:, None] == experts[None, :]).astype(jnp.int32)
    csum = jnp.cumsum(onehot, axis=0)
    counts = csum[-1]
    padded = ((counts + blk - 1) // blk) * blk
    start = jnp.cumsum(counts) - counts
    pend = jnp.cumsum(padded)
    pstart = pend - padded
    slot = jnp.sum(onehot * (csum - 1 + pstart[None, :]), axis=1)
    nb = -(-(n + N_EXPERTS * (blk - 1)) // blk)
    block_e = jnp.minimum(jnp.sum((pend // blk)[None, :] <= jnp.arange(nb, dtype=jnp.int32)[:, None], axis=1),
                          N_EXPERTS - 1).astype(jnp.int32)
    n_used = (pend[-1] // blk).astype(jnp.int32).reshape(1)
    order = jnp.argsort(flat_e).astype(jnp.int32)
    row_e = jnp.repeat(block_e, blk)
    row = jnp.arange(nb * blk, dtype=jnp.int32)
    off = row - pstart[row_e]
    src = jnp.clip(start[row_e] + off, 0, n - 1)
    buf_tok = jnp.where(off < counts[row_e], order[src] // TOP_K, 0)
    return buf_tok, slot.reshape(t, TOP_K), block_e, n_used


def _layer_weights(p, l):
    f = lambda name: p[name][l]
    zero_d = jnp.zeros((D_MODEL,), F32)
    has_vres = l > 0
    vecd = jnp.stack([f('norm_mix_w'), p['rw_mu_lora'][l, 0], p['rw_mu_lora'][l, 1], p['rw_mu_lora'][l, 2],
                      p['rw_mu_vres'][l - 1] if has_vres else zero_d, zero_d, zero_d, zero_d])
    zero_w = jnp.zeros((RW_WIDTH,), F32)
    vec5 = jnp.stack([f('rw_w0'), f('rw_a0'), p['rw_v0'][l - 1] if has_vres else zero_w,
                      f('rw_kk'), f('rw_ka'), zero_w, zero_w, zero_w])
    loras = [f('rw_w1'), f('rw_a1'), f('rw_g1')] + ([p['rw_v1'][l - 1]] if has_vres else [])
    loras += [f('rw_w2'), f('rw_a2'), f('rw_g2')] + ([p['rw_v2'][l - 1]] if has_vres else [])
    loras = [w.astype(BF16) for w in loras]
    rwvec = jnp.stack([f('rw_ln_w'), f('rw_ln_b'), f('rw_rk').reshape(RW_WIDTH),
                       zero_w, zero_w, zero_w, zero_w, zero_w])
    wr = jnp.concatenate([f('router_g_w'),
                          jnp.transpose(f('router_e_w'), (1, 0, 2)).reshape(D_MODEL, N_EXPERTS),
                          jnp.zeros((D_MODEL, ROUTE_W - N_GROUPS - N_EXPERTS), F32)], axis=1)
    br = jnp.concatenate([f('router_g_b'), f('router_e_b').reshape(N_EXPERTS),
                          jnp.zeros((ROUTE_W - N_GROUPS - N_EXPERTS,), F32)]).reshape(1, ROUTE_W)
    return dict(
        vecd=vecd, vec5=vec5, mup=f('rw_mu_proj').reshape(1, RW_PROJ), win=f('w_in').astype(BF16),
        loras=loras, rwvec=rwvec, hg_nw=f('hg_norm_w').reshape(1, HG_WIDTH),
        nfw=f('norm_ffn_w').reshape(1, D_MODEL), wout=f('w_out').astype(BF16), wr=wr, br=br)


def _trunk(x, mod, s_hg, s_rw, h_prev, lw_all, expert_w, lbvecs, final_w):
    b, l, d = x.shape
    depth = len(lw_all)
    tm = min(l, 256)
    ch = min(l, 256)
    c = min(l, CHUNK)
    blk = 256 if b * l * TOP_K >= 8192 else 32
    head = lax.broadcasted_iota(jnp.int32, (RW_WIDTH, RW_WIDTH), 0) // RW_HEAD
    seg = (head == head.T).astype(BF16)
    new_hg, new_rw, new_shift = [], [], []
    rwp_first = None
    zrow = jnp.zeros((b, d), F32)
    for li in range(depth):
        w = lw_all[li]
        sh1, sc1, g1, sh2, sc2, g2 = jnp.split(mod[li], 6, axis=-1)
        bvec1 = jnp.stack([sh1, sc1, h_prev[li], zrow, zrow, zrow, zrow, zrow], axis=1)
        bvec2 = jnp.stack([g1, sh2, sc2, g2, zrow, zrow, zrow, zrow], axis=1)
        hgraw, rwp, hlast = _premix(x, bvec1, w['vecd'], w['vec5'], w['mup'], w['win'], w['loras'],
                                    seg, rwp_first if li > 0 else None, tm)
        if li == 0:
            rwp_first = rwp
        ohg, hg_t = _hgrn(hgraw, lbvecs[li], w['hg_nw'], jnp.swapaxes(s_hg[li], -1, -2), ch, c)
        yrw, rw_fin = _rwkv(rwp, w['rwvec'], s_rw[li], ch, c)
        xn, h2, route = _postmix(ohg, yrw, x, bvec2, w['nfw'], w['wout'], w['wr'], w['br'], tm)
        t = b * l
        buf_tok, slot, block_e, n_used = _dispatch(route.reshape(t, 8), blk)
        xs = h2.reshape(t, d).at[buf_tok].get(mode='promise_in_bounds')
        yb = _experts(xs, block_e, n_used, *expert_w, li, blk)
        y0 = yb.at[slot[:, 0]].get(mode='promise_in_bounds').reshape(b, l, d)
        y1 = yb.at[slot[:, 1]].get(mode='promise_in_bounds').reshape(b, l, d)
        x = _combine(xn, y0, y1, route, bvec2, final_w, li == depth - 1, tm)
        new_hg.append(jnp.swapaxes(hg_t, -1, -2))
        new_rw.append(rw_fin)
        new_shift.append(hlast.reshape(b, d))
    return x, jnp.stack(new_hg), jnp.stack(new_rw), jnp.stack(new_shift)


def kernel(x_prompt, x_sample, state_hgrn, state_rwkv, state_shift, c_prompt, c_sample, w_ada, b_ada, norm_mix_w, norm_ffn_w, w_in, w_out, hg_lb_logits, hg_norm_w, rw_mu_proj, rw_mu_lora, rw_w0, rw_w1, rw_w2, rw_a0, rw_a1, rw_a2, rw_g1, rw_g2, rw_mu_vres, rw_v0, rw_v1, rw_v2, rw_kk, rw_ka, rw_rk, rw_ln_w, rw_ln_b, router_g_w, router_g_b, router_e_w, router_e_b, w_gate, w_up, w_down, final_norm_w):
    p = dict(norm_mix_w=norm_mix_w, norm_ffn_w=norm_ffn_w, w_in=w_in, w_out=w_out, hg_norm_w=hg_norm_w,
             rw_mu_proj=rw_mu_proj, rw_mu_lora=rw_mu_lora, rw_w0=rw_w0, rw_w1=rw_w1, rw_w2=rw_w2,
             rw_a0=rw_a0, rw_a1=rw_a1, rw_a2=rw_a2, rw_g1=rw_g1, rw_g2=rw_g2,
             rw_mu_vres=rw_mu_vres, rw_v0=rw_v0, rw_v1=rw_v1, rw_v2=rw_v2,
             rw_kk=rw_kk, rw_ka=rw_ka, rw_rk=rw_rk, rw_ln_w=rw_ln_w, rw_ln_b=rw_ln_b,
             router_g_w=router_g_w, router_g_b=router_g_b, router_e_w=router_e_w,
             router_e_b=router_e_b, w_gate=w_gate, w_up=w_up, w_down=w_down)
    depth = w_in.shape[0]
    bp = x_prompt.shape[0]
    bs = x_sample.shape[0]
    d = x_prompt.shape[-1]
    lw_all = [_layer_weights(p, l) for l in range(depth)]
    lb = jnp.cumsum(jax.nn.softmax(hg_lb_logits.astype(F32), axis=0), axis=0)
    lb = lb - lb[0]
    zpad = jnp.zeros((6, lb.shape[1]), F32)
    lbvecs = [jnp.concatenate([jnp.log(lb[l])[None], jnp.log1p(-lb[l])[None], zpad], axis=0)
              for l in range(depth)]
    rows = -(-(bp + bs) // 8) * 8
    c_all = jnp.concatenate([c_prompt, c_sample, jnp.zeros((rows - bp - bs, d), F32)], axis=0)
    mod = _ada_mod(c_all, w_ada, b_ada)
    fw = final_norm_w.reshape(1, d)
    zero_hg = jnp.zeros((depth, bp) + state_hgrn.shape[2:], F32)
    zero_rw = jnp.zeros((depth, bp) + state_rwkv.shape[2:], F32)
    zero_sh = jnp.zeros((depth, bp, d), F32)
    expert_w = (w_gate, w_up, w_down)
    y_p, hg_p, rw_p, sh_p = _trunk(x_prompt, mod[:, :bp], zero_hg, zero_rw, zero_sh, lw_all, expert_w,
                                   lbvecs, fw)
    y_s, hg_s, rw_s, sh_s = _trunk(x_sample, mod[:, bp:bp + bs], state_hgrn, state_rwkv, state_shift,
                                   lw_all, expert_w, lbvecs, fw)
    return (y_p, y_s, hg_p, rw_p, sh_p, hg_s, rw_s, sh_s)
```

```python
import functools

import jax
import jax.numpy as jnp
from jax import lax
from jax.experimental import pallas as pl
from jax.experimental.pallas import tpu as pltpu

F32 = jnp.float32
BF16 = jnp.bfloat16

D_MODEL = 1024
HG_HEADS = 4
HG_DK = 128
HG_DV = 128
HG_WIDTH = HG_HEADS * HG_DV
RW_HEADS = 8
RW_HEAD = 64
RW_WIDTH = RW_HEADS * RW_HEAD
HG_PROJ = 2 * HG_HEADS * HG_DK + 2 * HG_WIDTH
RW_PROJ = 3 * RW_WIDTH
IN_WIDTH = HG_PROJ + RW_PROJ
N_GROUPS = 4
EXPERTS_PER_GROUP = 8
N_EXPERTS = N_GROUPS * EXPERTS_PER_GROUP
TOP_K = 2
D_EXPERT = 512
LOG2E = 1.4426950408889634
NORM_EPS = 1e-6
RW_GN_EPS = 64e-5

CHUNK = 64
SUB = 16
HG_SUB = 8
ROUTE_W = 128
RW_PACK = 7 * RW_WIDTH
VMEM_LIMIT = 56 * 1024 * 1024

NN = (((1,), (0,)), ((), ()))
NT = (((1,), (1,)), ((), ()))
TN = (((0,), (0,)), ((), ()))


def _split2(a):
    hi = a.astype(BF16)
    lo = (a - hi.astype(F32)).astype(BF16)
    return hi, lo


def _mm(a, b, dims=NN, passes=1):
    if passes == 1:
        return lax.dot_general(a.astype(BF16), b.astype(BF16), dims, preferred_element_type=F32)
    a1, a2 = _split2(a)
    b1, b2 = _split2(b)
    out = lax.dot_general(a1, b1, dims, preferred_element_type=F32)
    out = out + lax.dot_general(a1, b2, dims, preferred_element_type=F32)
    return out + lax.dot_general(a2, b1, dims, preferred_element_type=F32)


def _mm_exact_lhs(a_bf, b):
    b1 = b.astype(BF16)
    r = b - b1.astype(F32)
    b2 = r.astype(BF16)
    b3 = (r - b2.astype(F32)).astype(BF16)
    out = jnp.dot(a_bf, b1, preferred_element_type=F32)
    out = out + jnp.dot(a_bf, b2, preferred_element_type=F32)
    return out + jnp.dot(a_bf, b3, preferred_element_type=F32)


def _sigmoid(x):
    return 1.0 / (1.0 + jnp.exp(-x))


def _silu(x):
    return x * _sigmoid(x)


def _softplus(x):
    return jnp.maximum(x, 0.0) + jnp.log1p(jnp.exp(-jnp.abs(x)))


def _tri_incl_bf16(n):
    r = lax.broadcasted_iota(jnp.int32, (n, n), 0)
    c = lax.broadcasted_iota(jnp.int32, (n, n), 1)
    return (r >= c).astype(BF16)


def _params(sem):
    return pltpu.CompilerParams(dimension_semantics=sem, vmem_limit_bytes=VMEM_LIMIT)


def _ada_kernel(c_ref, w_ref, b_ref, o_ref):
    cs = _silu(c_ref[...])
    o_ref[...] = _mm(cs, w_ref[...]) + b_ref[...]


def _ada_mod(c_all, w_ada, b_ada):
    depth, d, n = w_ada.shape
    rows = c_all.shape[0]
    tn = 1536
    return pl.pallas_call(
        _ada_kernel,
        grid=(depth, n // tn),
        in_specs=[
            pl.BlockSpec((rows, d), lambda l, j: (0, 0)),
            pl.BlockSpec((None, d, tn), lambda l, j: (l, 0, j)),
            pl.BlockSpec((None, 1, tn), lambda l, j: (l, 0, j)),
        ],
        out_specs=pl.BlockSpec((None, rows, tn), lambda l, j: (l, 0, j)),
        out_shape=jax.ShapeDtypeStruct((depth, rows, n), F32),
        compiler_params=_params(("parallel", "parallel")),
        name="ada_mod",
    )(c_all, w_ada, b_ada.reshape(depth, 1, n))


def _premix_kernel(has_vres, *refs):
    if has_vres:
        (x_ref, bvec_ref, vecd_ref, vec5_ref, mup_ref, win_ref, w1_ref, a1_ref, g1_ref, v1_ref,
         w2_ref, a2_ref, g2_ref, v2_ref, seg_ref, vfirst_ref,
         hg_ref, rw_ref, hlast_ref, hcar, pcar) = refs
    else:
        (x_ref, bvec_ref, vecd_ref, vec5_ref, mup_ref, win_ref, w1_ref, a1_ref, g1_ref,
         w2_ref, a2_ref, g2_ref, seg_ref,
         hg_ref, rw_ref, hlast_ref, hcar, pcar) = refs
    t = pl.program_id(1)
    tm = x_ref.shape[0]
    w_rw = win_ref[:, HG_PROJ:]

    @pl.when(t == 0)
    def _():
        hp = jnp.broadcast_to(bvec_ref[2:3, :], (8, D_MODEL))
        hcar[...] = hp
        pcar[...] = _mm(hp, w_rw)

    x = x_ref[...]
    nw = vecd_ref[0:1, :]
    sh = bvec_ref[0:1, :]
    sc = bvec_ref[1:2, :]
    xn = x * lax.rsqrt(jnp.mean(x * x, axis=-1, keepdims=True) + NORM_EPS) * nw
    h = xn * (1.0 + sc) + sh

    hb = h.astype(BF16)
    hg_ref[...] = jnp.dot(hb, win_ref[:, :HG_PROJ], preferred_element_type=F32)
    p_cur = jnp.dot(hb, w_rw, preferred_element_type=F32)

    first = lax.broadcasted_iota(jnp.int32, (tm, 1), 0) == 0
    h_shift = jnp.where(first, hcar[0:1, :], pltpu.roll(h, 1, 0))
    p_shift = jnp.where(first, pcar[0:1, :], pltpu.roll(p_cur, 1, 0))
    hcar[0:1, :] = h[tm - 1:tm, :]
    pcar[0:1, :] = p_cur[tm - 1:tm, :]
    hlast_ref[...] = h[tm - 1:tm, :]

    pm = p_cur + (p_shift - p_cur) * mup_ref[...]
    r = pm[:, :RW_WIDTH]
    k = pm[:, RW_WIDTH:2 * RW_WIDTH]
    v = pm[:, 2 * RW_WIDTH:]

    dh = h_shift - h
    xw = h + dh * vecd_ref[1:2, :]
    xa = h + dh * vecd_ref[2:3, :]
    xg = h + dh * vecd_ref[3:4, :]
    w_pre = vec5_ref[0:1, :] + _mm(jnp.tanh(_mm(xw, w1_ref[...])), w2_ref[...])
    logw = -jnp.exp(-_softplus(-w_pre) - 0.5)
    a = _sigmoid(vec5_ref[1:2, :] + _mm(_mm(xa, a1_ref[...]), a2_ref[...]))
    g = _mm(_sigmoid(_mm(xg, g1_ref[...])), g2_ref[...])
    if has_vres:
        xv = h + dh * vecd_ref[4:5, :]
        nu = _sigmoid(vec5_ref[2:3, :] + _mm(_mm(xv, v1_ref[...]), v2_ref[...]))
        v = v + (vfirst_ref[...] - v) * nu

    kk = k * vec5_ref[3:4, :]
    ss = _mm(kk * kk, seg_ref[...], passes=3)
    kappa = kk / jnp.maximum(jnp.sqrt(ss), 1e-12)
    k_mod = k * (1.0 + (a - 1.0) * vec5_ref[4:5, :])

    rw_ref[:, 0 * RW_WIDTH:1 * RW_WIDTH] = r
    rw_ref[:, 1 * RW_WIDTH:2 * RW_WIDTH] = logw
    rw_ref[:, 2 * RW_WIDTH:3 * RW_WIDTH] = k_mod
    rw_ref[:, 3 * RW_WIDTH:4 * RW_WIDTH] = v
    rw_ref[:, 4 * RW_WIDTH:5 * RW_WIDTH] = kappa
    rw_ref[:, 5 * RW_WIDTH:6 * RW_WIDTH] = kappa * a
    rw_ref[:, 6 * RW_WIDTH:7 * RW_WIDTH] = g


def _premix(x, bvec, vecd, vec5, mup, win, loras, seg, vfirst, tm):
    b, l, d = x.shape
    has_vres = vfirst is not None
    const = lambda shape: pl.BlockSpec(shape, lambda i, j: tuple(0 for _ in shape))
    in_specs = [
        pl.BlockSpec((None, tm, d), lambda i, j: (i, j, 0)),
        pl.BlockSpec((None, 8, d), lambda i, j: (i, 0, 0)),
        const(vecd.shape), const(vec5.shape), const(mup.shape), const(win.shape),
    ]
    in_specs += [const(w.shape) for w in loras]
    in_specs.append(const(seg.shape))
    args = [x, bvec, vecd, vec5, mup, win, *loras, seg]
    if has_vres:
        in_specs.append(pl.BlockSpec((None, tm, RW_WIDTH), lambda i, j: (i, j, 3)))
        args.append(vfirst)
    return pl.pallas_call(
        functools.partial(_premix_kernel, has_vres),
        grid=(b, l // tm),
        in_specs=in_specs,
        out_specs=[
            pl.BlockSpec((None, tm, HG_PROJ), lambda i, j: (i, j, 0)),
            pl.BlockSpec((None, tm, RW_PACK), lambda i, j: (i, j, 0)),
            pl.BlockSpec((None, 1, d), lambda i, j: (i, 0, 0)),
        ],
        out_shape=[
            jax.ShapeDtypeStruct((b, l, HG_PROJ), F32),
            jax.ShapeDtypeStruct((b, l, RW_PACK), F32),
            jax.ShapeDtypeStruct((b, 1, d), F32),
        ],
        scratch_shapes=[pltpu.VMEM((8, d), F32), pltpu.VMEM((8, RW_PROJ), F32)],
        compiler_params=_params(("parallel", "arbitrary")),
        name="premix",
    )(*args)


def _hgrn_chunk(zq, zf, zi, zg, st, loglb, log1mlb, nw, tri):
    c = zq.shape[0]
    q = _silu(zq)
    bterm = log1mlb + (jnp.minimum(zf, 0.0) - jnp.log1p(jnp.exp(-jnp.abs(zf))))
    logf = jnp.maximum(loglb, bterm) + jnp.log1p(jnp.exp(-jnp.abs(loglb - bterm)))
    k = 1.0 - jnp.exp(logf)
    v = zi
    bcum = _mm_exact_lhs(tri, logf) * LOG2E
    row = lax.broadcasted_iota(jnp.int32, (c, 1), 0)
    rr = lax.broadcasted_iota(jnp.int32, (c, c), 0)
    cc = lax.broadcasted_iota(jnp.int32, (c, c), 1)
    a = jnp.zeros((c, c), F32)
    m = c // 2
    while m >= HG_SUB:
        bmid = jnp.concatenate([jnp.broadcast_to(bcum[(2 * j + 1) * m - 1:(2 * j + 1) * m], (2 * m, HG_DK))
                                for j in range(c // (2 * m))], axis=0)
        upper = ((row // m) % 2) == 1
        qt = jnp.where(upper, q * jnp.exp2(bcum - bmid), 0.0)
        kt = jnp.where(upper, 0.0, k * jnp.exp2(bmid - bcum))
        a = a + jnp.where((rr // (2 * m)) == (cc // (2 * m)), _mm(qt, kt, NT), 0.0)
        m //= 2
    lane = lax.broadcasted_iota(jnp.int32, (HG_SUB, c), 1)
    rws = lax.broadcasted_iota(jnp.int32, (HG_SUB, c), 0)
    diag = []
    for i in range(c // HG_SUB):
        lo = i * HG_SUB
        bi = bcum[lo:lo + HG_SUB]
        qi = q[lo:lo + HG_SUB]
        ki = k[lo:lo + HG_SUB]
        d = jnp.zeros((HG_SUB, c), F32)
        for s in range(HG_SUB):
            dec = jnp.exp2(bi - bi[s:s + 1])
            col = jnp.sum(qi * ki[s:s + 1] * dec, axis=-1, keepdims=True)
            d = jnp.where((lane == lo + s) & (rws >= s), col, d)
        diag.append(d)
    a = a + jnp.concatenate(diag, axis=0)
    o = _mm(q * jnp.exp2(bcum), st, NT) + _mm(a, v)
    b_last = bcum[c - 1:c]
    st_new = st * jnp.exp2(b_last) + _mm(v, k * jnp.exp2(b_last - bcum), TN)
    on = o * lax.rsqrt(jnp.mean(o * o, axis=-1, keepdims=True) + NORM_EPS) * nw
    return on * _silu(zg), st_new


def _hgrn_kernel(c, hg_ref, lb_ref, nw_ref, s0_ref, o_ref, sfin_ref):
    t = pl.program_id(1)

    @pl.when(t == 0)
    def _():
        sfin_ref[...] = s0_ref[...]

    tri = _tri_incl_bf16(c)
    qk = HG_HEADS * HG_DK
    for ci in range(hg_ref.shape[0] // c):
        r0 = ci * c
        for h in range(HG_HEADS):
            zq = hg_ref[r0:r0 + c, h * HG_DK:(h + 1) * HG_DK]
            zf = hg_ref[r0:r0 + c, qk + h * HG_DK:qk + (h + 1) * HG_DK]
            zi = hg_ref[r0:r0 + c, 2 * qk + h * HG_DV:2 * qk + (h + 1) * HG_DV]
            zg = hg_ref[r0:r0 + c, 2 * qk + HG_WIDTH + h * HG_DV:2 * qk + HG_WIDTH + (h + 1) * HG_DV]
            sl = slice(h * HG_DK, (h + 1) * HG_DK)
            out, st_new = _hgrn_chunk(zq, zf, zi, zg, sfin_ref[h], lb_ref[0:1, sl], lb_ref[1:2, sl],
                                      nw_ref[0:1, h * HG_DV:(h + 1) * HG_DV], tri)
            sfin_ref[h] = st_new
            o_ref[r0:r0 + c, h * HG_DV:(h + 1) * HG_DV] = out


def _hgrn(hgraw, lbvec, nw, s0t, ch, c):
    b, l, _ = hgraw.shape
    return pl.pallas_call(
        functools.partial(_hgrn_kernel, c),
        grid=(b, l // ch),
        in_specs=[
            pl.BlockSpec((None, ch, HG_PROJ), lambda i, j: (i, j, 0)),
            pl.BlockSpec(lbvec.shape, lambda i, j: (0, 0)),
            pl.BlockSpec(nw.shape, lambda i, j: (0, 0)),
            pl.BlockSpec((None, HG_HEADS, HG_DV, HG_DK), lambda i, j: (i, 0, 0, 0)),
        ],
        out_specs=[
            pl.BlockSpec((None, ch, HG_WIDTH), lambda i, j: (i, j, 0)),
            pl.BlockSpec((None, HG_HEADS, HG_DV, HG_DK), lambda i, j: (i, 0, 0, 0)),
        ],
        out_shape=[
            jax.ShapeDtypeStruct((b, l, HG_WIDTH), F32),
            jax.ShapeDtypeStruct((b, HG_HEADS, HG_DV, HG_DK), F32),
        ],
        compiler_params=_params(("parallel", "arbitrary")),
        name="hgrn",
    )(hgraw, lbvec, nw, s0t)


RW_PASSES_A = 1
RW_PASSES_B = 1
RW_GROUP = 2


def _rwkv_chunk_matrices(c, chunks, tri, strict, incl, same_sub):
    pa = RW_PASSES_A
    prep = []
    for r, lw, km, v, kap, beta in chunks:
        cum = _mm_exact_lhs(tri, lw)
        c_last = cum[c - 1:c]
        e_neg = jnp.exp(-cum)
        e_hat = jnp.exp(c_last - cum)
        prep.append(dict(kbar=kap * jnp.exp(cum - lw), rbar=r * jnp.exp(cum), bneg=beta * e_neg,
                         kneg=km * e_neg, bhat=beta * e_hat, khat=km * e_hat, v=v, e_last=jnp.exp(c_last)))
    e_lasts = [pr['e_last'] for pr in prep]
    streams = [(ci, h) for ci in range(len(chunks)) for h in range(RW_HEADS)]
    hs = range(len(streams))

    def part(name):
        return [prep[ci][name][:, h * RW_HEAD:(h + 1) * RW_HEAD] for ci, h in streams]

    kbar, rbar, bneg, kneg, bhat, khat, vh = (part(n) for n in ('kbar', 'rbar', 'bneg', 'kneg', 'bhat', 'khat', 'v'))
    mask_kq = jnp.concatenate([strict, incl], axis=0)
    lhs = [jnp.concatenate([kbar[h], rbar[h]], axis=0) for h in hs]
    rhs = [jnp.concatenate([bneg[h], kneg[h]], axis=0) for h in hs]
    gmat = [_mm(lhs[h], rhs[h], NT, passes=pa) for h in hs]
    aab = [jnp.where(strict, gmat[h][:c, :c], 0.0) for h in hs]
    aqb = [jnp.where(incl, gmat[h][c:, :c], 0.0) for h in hs]
    akq = [jnp.where(mask_kq, gmat[h][:, c:], 0.0) for h in hs]
    av = [_mm(akq[h], vh[h], passes=pa) for h in hs]
    d1 = [jnp.where(same_sub, aab[h], 0.0) for h in hs]
    if c == SUB:
        z = [jnp.concatenate([kbar[h], av[h][:c]], axis=1) for h in hs]
    else:
        z = [jnp.concatenate([aab[h] - d1[h], kbar[h], av[h][:c]], axis=1) for h in hs]
    d2 = [_mm(d1[h], d1[h], passes=pa) for h in hs]
    z = [z[h] - _mm(d1[h], z[h], passes=pa) for h in hs]
    d4 = [_mm(d2[h], d2[h], passes=pa) for h in hs]
    z = [z[h] + _mm(d2[h], z[h], passes=pa) for h in hs]
    d8 = [_mm(d4[h], d4[h], passes=pa) for h in hs]
    z = [z[h] + _mm(d4[h], z[h], passes=pa) for h in hs]
    z = [z[h] + _mm(d8[h], z[h], passes=pa) for h in hs]
    if c == SUB:
        x = z
    else:
        f = [z[h][:, :c] for h in hs]
        x = [z[h][:, c:] for h in hs]
        fx = [_mm(f[h], x[h], passes=pa) for h in hs]
        x = [x[h] + _mm(f[h], fx[h], passes=pa) for h in hs]
        x = [x[h] - _mm(f[h], x[h], passes=pa) for h in hs]
    qb = [_mm(aqb[h], x[h], passes=pa) for h in hs]
    rt = [rbar[h] - qb[h][:, :RW_HEAD] for h in hs]
    yc = [av[h][c:] - qb[h][:, RW_HEAD:] for h in hs]
    ktb = [_mm(x[h][:, :RW_HEAD], bhat[h], TN, passes=pa) for h in hs]
    nc = [_mm(jnp.concatenate([vh[h], x[h][:, RW_HEAD:]], axis=0),
              jnp.concatenate([khat[h], -bhat[h]], axis=0), TN, passes=pa) for h in hs]
    return rt, yc, ktb, nc, e_lasts


def _rwkv_kernel(c, rw_ref, vec_ref, s0_ref, o_ref, sfin_ref):
    t = pl.program_id(1)

    @pl.when(t == 0)
    def _():
        sfin_ref[...] = s0_ref[...]

    tri = _tri_incl_bf16(c)
    row = lax.broadcasted_iota(jnp.int32, (c, c), 0)
    col = lax.broadcasted_iota(jnp.int32, (c, c), 1)
    strict = row > col
    incl = row >= col
    same_sub = (row // SUB) == (col // SUB)
    w = RW_WIDTH
    hs = range(RW_HEADS)
    sl = [slice(h * RW_HEAD, (h + 1) * RW_HEAD) for h in hs]
    state = [sfin_ref[h] for h in hs]
    n_chunks = rw_ref.shape[0] // c
    for g0 in range(0, n_chunks, RW_GROUP):
        group = range(g0, min(g0 + RW_GROUP, n_chunks))
        chunks = [tuple(rw_ref[ci * c:(ci + 1) * c, j * w:(j + 1) * w] for j in range(6)) for ci in group]
        rt, yc, ktb, nc, e_lasts = _rwkv_chunk_matrices(c, chunks, tri, strict, incl, same_sub)
        for gi, ci in enumerate(group):
            r0 = ci * c
            r, _, km, v, _, _ = chunks[gi]
            gate = rw_ref[r0:r0 + c, 6 * w:7 * w]
            q = gi * RW_HEADS
            y = [_mm(rt[q + h], state[h], NT, passes=RW_PASSES_B) + yc[q + h] for h in hs]
            state = [state[h] * e_lasts[gi][:, sl[h]] - _mm(state[h], ktb[q + h], passes=RW_PASSES_B) + nc[q + h]
                     for h in hs]
            bonus = r * km * vec_ref[2:3, :]
            for h in hs:
                mean = jnp.mean(y[h], axis=-1, keepdims=True)
                yd = y[h] - mean
                var = jnp.mean(yd * yd, axis=-1, keepdims=True)
                yn = yd * lax.rsqrt(var + RW_GN_EPS) * vec_ref[0:1, sl[h]] + vec_ref[1:2, sl[h]]
                yn = yn + jnp.sum(bonus[:, sl[h]], axis=-1, keepdims=True) * v[:, sl[h]]
                o_ref[r0:r0 + c, sl[h]] = yn * gate[:, sl[h]]
    for h in hs:
        sfin_ref[h] = state[h]


def _rwkv(rwp, vec, s0, ch, c):
    b, l, _ = rwp.shape
    return pl.pallas_call(
        functools.partial(_rwkv_kernel, c),
        grid=(b, l // ch),
        in_specs=[
            pl.BlockSpec((None, ch, RW_PACK), lambda i, j: (i, j, 0)),
            pl.BlockSpec(vec.shape, lambda i, j: (0, 0)),
            pl.BlockSpec((None, RW_HEADS, RW_HEAD, RW_HEAD), lambda i, j: (i, 0, 0, 0)),
        ],
        out_specs=[
            pl.BlockSpec((None, ch, RW_WIDTH), lambda i, j: (i, j, 0)),
            pl.BlockSpec((None, RW_HEADS, RW_HEAD, RW_HEAD), lambda i, j: (i, 0, 0, 0)),
        ],
        out_shape=[
            jax.ShapeDtypeStruct((b, l, RW_WIDTH), F32),
            jax.ShapeDtypeStruct((b, RW_HEADS, RW_HEAD, RW_HEAD), F32),
        ],
        compiler_params=_params(("parallel", "arbitrary")),
        name="rwkv",
    )(rwp, vec, s0)


def _postmix_kernel(ohg_ref, yrw_ref, x_ref, bvec_ref, nw_ref, wout_ref, wr_ref, br_ref,
                    xn_ref, h2_ref, route_ref):
    m = (jnp.dot(ohg_ref[...].astype(BF16), wout_ref[:HG_WIDTH, :], preferred_element_type=F32)
         + jnp.dot(yrw_ref[...].astype(BF16), wout_ref[HG_WIDTH:, :], preferred_element_type=F32))
    xn = x_ref[...] + bvec_ref[0:1, :] * m
    xn_ref[...] = xn
    h2 = (xn * lax.rsqrt(jnp.mean(xn * xn, axis=-1, keepdims=True) + NORM_EPS) * nw_ref[...]
          * (1.0 + bvec_ref[2:3, :]) + bvec_ref[1:2, :])
    h2_ref[...] = h2.astype(BF16)

    logits = _mm(h2, wr_ref[...]) + br_ref[...]
    lane = lax.broadcasted_iota(jnp.int32, logits.shape, 1)
    neg = jnp.float32(-jnp.inf)
    big = jnp.int32(ROUTE_W)
    is_g = lane < N_GROUPS
    gl = jnp.where(is_g, logits, neg)
    gmax = jnp.max(gl, axis=-1, keepdims=True)
    grp = jnp.min(jnp.where(gl == gmax, lane, big), axis=-1, keepdims=True)
    p_grp = 1.0 / jnp.sum(jnp.where(is_g, jnp.exp(gl - gmax), 0.0), axis=-1, keepdims=True)
    e_lo = N_GROUPS + grp * EXPERTS_PER_GROUP
    in_grp = (lane >= e_lo) & (lane < e_lo + EXPERTS_PER_GROUP)
    el = jnp.where(in_grp, logits, neg)
    m1 = jnp.max(el, axis=-1, keepdims=True)
    i1 = jnp.min(jnp.where(el == m1, lane, big), axis=-1, keepdims=True)
    el2 = jnp.where(lane == i1, neg, el)
    m2 = jnp.max(el2, axis=-1, keepdims=True)
    i2 = jnp.min(jnp.where(el2 == m2, lane, big), axis=-1, keepdims=True)
    e21 = jnp.exp(m2 - m1)
    p1 = 1.0 / (1.0 + e21)
    p2 = e21 / (1.0 + e21)
    out = jnp.where(lane == 0, (i1 - N_GROUPS).astype(F32),
                    jnp.where(lane == 1, (i2 - N_GROUPS).astype(F32),
                              jnp.where(lane == 2, p_grp * p1,
                                        jnp.where(lane == 3, p_grp * p2, 0.0))))
    route_ref[...] = out[:, :8]


def _postmix(ohg, yrw, x, bvec, nw, wout, wr, br, tm):
    b, l, d = x.shape
    const = lambda shape: pl.BlockSpec(shape, lambda i, j: tuple(0 for _ in shape))
    return pl.pallas_call(
        _postmix_kernel,
        grid=(b, l // tm),
        in_specs=[
            pl.BlockSpec((None, tm, HG_WIDTH), lambda i, j: (i, j, 0)),
            pl.BlockSpec((None, tm, RW_WIDTH), lambda i, j: (i, j, 0)),
            pl.BlockSpec((None, tm, d), lambda i, j: (i, j, 0)),
            pl.BlockSpec((None, 8, d), lambda i, j: (i, 0, 0)),
            const(nw.shape), const(wout.shape), const(wr.shape), const(br.shape),
        ],
        out_specs=[
            pl.BlockSpec((None, tm, d), lambda i, j: (i, j, 0)),
            pl.BlockSpec((None, tm, d), lambda i, j: (i, j, 0)),
            pl.BlockSpec((None, tm, 8), lambda i, j: (i, j, 0)),
        ],
        out_shape=[
            jax.ShapeDtypeStruct((b, l, d), F32),
            jax.ShapeDtypeStruct((b, l, d), BF16),
            jax.ShapeDtypeStruct((b, l, 8), F32),
        ],
        compiler_params=_params(("parallel", "parallel")),
        name="postmix",
    )(ohg, yrw, x, bvec, nw, wout, wr, br)


def _expert_kernel(be_ref, nu_ref, xs_ref, wg_ref, wu_ref, wd_ref, o_ref):
    i = pl.program_id(0)

    @pl.when(i < nu_ref[0])
    def _():
        xb = xs_ref[...].astype(BF16)
        hg = jnp.dot(xb, wg_ref[...].astype(BF16), preferred_element_type=F32)
        hu = jnp.dot(xb, wu_ref[...].astype(BF16), preferred_element_type=F32)
        hdn = (_silu(hg) * hu).astype(BF16)
        o_ref[...] = jnp.dot(hdn, wd_ref[...].astype(BF16), preferred_element_type=F32)

    @pl.when(i >= nu_ref[0])
    def _():
        o_ref[...] = jnp.zeros_like(o_ref)


def _experts(xs, block_e, n_used, wg, wu, wd, layer, blk):
    p, d = xs.shape
    nb = p // blk
    grid_spec = pltpu.PrefetchScalarGridSpec(
        num_scalar_prefetch=2,
        grid=(nb,),
        in_specs=[
            pl.BlockSpec((blk, d), lambda i, be, nu: (i, 0)),
            pl.BlockSpec((None, None, d, D_EXPERT), lambda i, be, nu: (layer, be[i], 0, 0)),
            pl.BlockSpec((None, None, d, D_EXPERT), lambda i, be, nu: (layer, be[i], 0, 0)),
            pl.BlockSpec((None, None, D_EXPERT, d), lambda i, be, nu: (layer, be[i], 0, 0)),
        ],
        out_specs=pl.BlockSpec((blk, d), lambda i, be, nu: (i, 0)),
    )
    return pl.pallas_call(
        _expert_kernel,
        grid_spec=grid_spec,
        out_shape=jax.ShapeDtypeStruct((p, d), F32),
        compiler_params=_params(("arbitrary",)),
        name="experts",
    )(block_e, n_used, xs, wg, wu, wd)


def _combine_kernel(final, xn_ref, y0_ref, y1_ref, route_ref, bvec_ref, fw_ref, o_ref):
    rt = route_ref[...]
    moe = rt[:, 2:3] * y0_ref[...] + rt[:, 3:4] * y1_ref[...]
    x = xn_ref[...] + bvec_ref[3:4, :] * moe
    if final:
        x = x * lax.rsqrt(jnp.mean(x * x, axis=-1, keepdims=True) + NORM_EPS) * fw_ref[...]
    o_ref[...] = x


def _combine(xn, y0, y1, route, bvec, fw, final, tm):
    b, l, d = xn.shape
    row = lambda w: pl.BlockSpec((None, tm, w), lambda i, j: (i, j, 0))
    return pl.pallas_call(
        functools.partial(_combine_kernel, final),
        grid=(b, l // tm),
        in_specs=[row(d), row(d), row(d), row(8),
                  pl.BlockSpec((None, 8, d), lambda i, j: (i, 0, 0)),
                  pl.BlockSpec((1, d), lambda i, j: (0, 0))],
        out_specs=row(d),
        out_shape=jax.ShapeDtypeStruct((b, l, d), F32),
        compiler_params=_params(("parallel", "parallel")),
        name="combine",
    )(xn, y0, y1, route, bvec, fw)


def _dispatch(route, blk):
    t = route.shape[0]
    n = t * TOP_K
    flat_e = route[:, :TOP_K].astype(jnp.int32).reshape(n)
    experts = jnp.arange(N_EXPERTS, dtype=jnp.int32)
    onehot = (flat_e[:, None] == experts[None, :]).astype(jnp.int32)
    csum = jnp.cumsum(onehot, axis=0)
    counts = csum[-1]
    padded = ((counts + blk - 1) // blk) * blk
    start = jnp.cumsum(counts) - counts
    pend = jnp.cumsum(padded)
    pstart = pend - padded
    slot = jnp.sum(onehot * (csum - 1 + pstart[None, :]), axis=1)
    nb = -(-(n + N_EXPERTS * (blk - 1)) // blk)
    block_e = jnp.minimum(jnp.sum((pend // blk)[None, :] <= jnp.arange(nb, dtype=jnp.int32)[:, None], axis=1),
                          N_EXPERTS - 1).astype(jnp.int32)
    n_used = (pend[-1] // blk).astype(jnp.int32).reshape(1)
    order = jnp.argsort(flat_e).astype(jnp.int32)
    row_e = jnp.repeat(block_e, blk)
    row = jnp.arange(nb * blk, dtype=jnp.int32)
    off = row - pstart[row_e]
    src = jnp.clip(start[row_e] + off, 0, n - 1)
    buf_tok = jnp.where(off < counts[row_e], order[src] // TOP_K, row % t)
    return buf_tok, slot.reshape(t, TOP_K), block_e, n_used


def _layer_weights(p, l):
    f = lambda name: p[name][l]
    zero_d = jnp.zeros((D_MODEL,), F32)
    has_vres = l > 0
    vecd = jnp.stack([f('norm_mix_w'), p['rw_mu_lora'][l, 0], p['rw_mu_lora'][l, 1], p['rw_mu_lora'][l, 2],
                      p['rw_mu_vres'][l - 1] if has_vres else zero_d, zero_d, zero_d, zero_d])
    zero_w = jnp.zeros((RW_WIDTH,), F32)
    vec5 = jnp.stack([f('rw_w0'), f('rw_a0'), p['rw_v0'][l - 1] if has_vres else zero_w,
                      f('rw_kk'), f('rw_ka'), zero_w, zero_w, zero_w])
    loras = [f('rw_w1'), f('rw_a1'), f('rw_g1')] + ([p['rw_v1'][l - 1]] if has_vres else [])
    loras += [f('rw_w2'), f('rw_a2'), f('rw_g2')] + ([p['rw_v2'][l - 1]] if has_vres else [])
    loras = [w.astype(BF16) for w in loras]
    rwvec = jnp.stack([f('rw_ln_w'), f('rw_ln_b'), f('rw_rk').reshape(RW_WIDTH),
                       zero_w, zero_w, zero_w, zero_w, zero_w])
    wr = jnp.concatenate([f('router_g_w'),
                          jnp.transpose(f('router_e_w'), (1, 0, 2)).reshape(D_MODEL, N_EXPERTS),
                          jnp.zeros((D_MODEL, ROUTE_W - N_GROUPS - N_EXPERTS), F32)], axis=1)
    br = jnp.concatenate([f('router_g_b'), f('router_e_b').reshape(N_EXPERTS),
                          jnp.zeros((ROUTE_W - N_GROUPS - N_EXPERTS,), F32)]).reshape(1, ROUTE_W)
    return dict(
        vecd=vecd, vec5=vec5, mup=f('rw_mu_proj').reshape(1, RW_PROJ), win=f('w_in').astype(BF16),
        loras=loras, rwvec=rwvec, hg_nw=f('hg_norm_w').reshape(1, HG_WIDTH),
        nfw=f('norm_ffn_w').reshape(1, D_MODEL), wout=f('w_out').astype(BF16), wr=wr, br=br)


def _trunk(x, mod, s_hg, s_rw, h_prev, lw_all, expert_w, lbvecs, final_w):
    b, l, d = x.shape
    depth = len(lw_all)
    tm = min(l, 256)
    ch = min(l, 256)
    c = min(l, CHUNK)
    blk = 512 if b * l * TOP_K >= 8192 else 32
    head = lax.broadcasted_iota(jnp.int32, (RW_WIDTH, RW_WIDTH), 0) // RW_HEAD
    seg = (head == head.T).astype(BF16)
    new_hg, new_rw, new_shift = [], [], []
    rwp_first = None
    zrow = jnp.zeros((b, d), F32)
    for li in range(depth):
        w = lw_all[li]
        sh1, sc1, g1, sh2, sc2, g2 = jnp.split(mod[li], 6, axis=-1)
        bvec1 = jnp.stack([sh1, sc1, h_prev[li], zrow, zrow, zrow, zrow, zrow], axis=1)
        bvec2 = jnp.stack([g1, sh2, sc2, g2, zrow, zrow, zrow, zrow], axis=1)
        hgraw, rwp, hlast = _premix(x, bvec1, w['vecd'], w['vec5'], w['mup'], w['win'], w['loras'],
                                    seg, rwp_first if li > 0 else None, tm)
        if li == 0:
            rwp_first = rwp
        ohg, hg_t = _hgrn(hgraw, lbvecs[li], w['hg_nw'], jnp.swapaxes(s_hg[li], -1, -2), ch, c)
        yrw, rw_fin = _rwkv(rwp, w['rwvec'], s_rw[li], ch, c)
        xn, h2, route = _postmix(ohg, yrw, x, bvec2, w['nfw'], w['wout'], w['wr'], w['br'], tm)
        t = b * l
        buf_tok, slot, block_e, n_used = _dispatch(route.reshape(t, 8), blk)
        xs = h2.reshape(t, d).at[buf_tok].get(mode='promise_in_bounds')
        yb = _experts(xs, block_e, n_used, *expert_w, li, blk)
        y0 = yb.at[slot[:, 0]].get(mode='promise_in_bounds').reshape(b, l, d)
        y1 = yb.at[slot[:, 1]].get(mode='promise_in_bounds').reshape(b, l, d)
        x = _combine(xn, y0, y1, route, bvec2, final_w, li == depth - 1, tm)
        new_hg.append(jnp.swapaxes(hg_t, -1, -2))
        new_rw.append(rw_fin)
        new_shift.append(hlast.reshape(b, d))
    return x, jnp.stack(new_hg), jnp.stack(new_rw), jnp.stack(new_shift)


def kernel(x_prompt, x_sample, state_hgrn, state_rwkv, state_shift, c_prompt, c_sample, w_ada, b_ada, norm_mix_w, norm_ffn_w, w_in, w_out, hg_lb_logits, hg_norm_w, rw_mu_proj, rw_mu_lora, rw_w0, rw_w1, rw_w2, rw_a0, rw_a1, rw_a2, rw_g1, rw_g2, rw_mu_vres, rw_v0, rw_v1, rw_v2, rw_kk, rw_ka, rw_rk, rw_ln_w, rw_ln_b, router_g_w, router_g_b, router_e_w, router_e_b, w_gate, w_up, w_down, final_norm_w):
    p = dict(norm_mix_w=norm_mix_w, norm_ffn_w=norm_ffn_w, w_in=w_in, w_out=w_out, hg_norm_w=hg_norm_w,
             rw_mu_proj=rw_mu_proj, rw_mu_lora=rw_mu_lora, rw_w0=rw_w0, rw_w1=rw_w1, rw_w2=rw_w2,
             rw_a0=rw_a0, rw_a1=rw_a1, rw_a2=rw_a2, rw_g1=rw_g1, rw_g2=rw_g2,
             rw_mu_vres=rw_mu_vres, rw_v0=rw_v0, rw_v1=rw_v1, rw_v2=rw_v2,
             rw_kk=rw_kk, rw_ka=rw_ka, rw_rk=rw_rk, rw_ln_w=rw_ln_w, rw_ln_b=rw_ln_b,
             router_g_w=router_g_w, router_g_b=router_g_b, router_e_w=router_e_w,
             router_e_b=router_e_b, w_gate=w_gate, w_up=w_up, w_down=w_down)
    depth = w_in.shape[0]
    bp = x_prompt.shape[0]
    bs = x_sample.shape[0]
    d = x_prompt.shape[-1]
    lw_all = [_layer_weights(p, l) for l in range(depth)]
    lb = jnp.cumsum(jax.nn.softmax(hg_lb_logits.astype(F32), axis=0), axis=0)
    lb = lb - lb[0]
    zpad = jnp.zeros((6, lb.shape[1]), F32)
    lbvecs = [jnp.concatenate([jnp.log(lb[l])[None], jnp.log1p(-lb[l])[None], zpad], axis=0)
              for l in range(depth)]
    rows = -(-(bp + bs) // 8) * 8
    c_all = jnp.concatenate([c_prompt, c_sample, jnp.zeros((rows - bp - bs, d), F32)], axis=0)
    mod = _ada_mod(c_all, w_ada, b_ada)
    fw = final_norm_w.reshape(1, d)
    zero_hg = jnp.zeros((depth, bp) + state_hgrn.shape[2:], F32)
    zero_rw = jnp.zeros((depth, bp) + state_rwkv.shape[2:], F32)
    zero_sh = jnp.zeros((depth, bp, d), F32)
    expert_w = (w_gate, w_up, w_down)
    y_p, hg_p, rw_p, sh_p = _trunk(x_prompt, mod[:, :bp], zero_hg, zero_rw, zero_sh, lw_all, expert_w,
                                   lbvecs, fw)
    y_s, hg_s, rw_s, sh_s = _trunk(x_sample, mod[:, bp:bp + bs], state_hgrn, state_rwkv, state_shift,
                                   lw_all, expert_w, lbvecs, fw)
    return (y_p, y_s, hg_p, rw_p, sh_p, hg_s, rw_s, sh_s)
```

```python
import functools

import jax
import jax.numpy as jnp
from jax import lax
from jax.experimental import pallas as pl
from jax.experimental.pallas import tpu as pltpu

F32 = jnp.float32
BF16 = jnp.bfloat16

D_MODEL = 1024
HG_HEADS = 4
HG_DK = 128
HG_DV = 128
HG_WIDTH = HG_HEADS * HG_DV
RW_HEADS = 8
RW_HEAD = 64
RW_WIDTH = RW_HEADS * RW_HEAD
HG_PROJ = 2 * HG_HEADS * HG_DK + 2 * HG_WIDTH
RW_PROJ = 3 * RW_WIDTH
IN_WIDTH = HG_PROJ + RW_PROJ
N_GROUPS = 4
EXPERTS_PER_GROUP = 8
N_EXPERTS = N_GROUPS * EXPERTS_PER_GROUP
TOP_K = 2
D_EXPERT = 512
LOG2E = 1.4426950408889634
NORM_EPS = 1e-6
RW_GN_EPS = 64e-5

CHUNK = 64
SUB = 16
HG_SUB = 8
ROUTE_W = 128
RW_PACK = 7 * RW_WIDTH
VMEM_LIMIT = 56 * 1024 * 1024

NN = (((1,), (0,)), ((), ()))
NT = (((1,), (1,)), ((), ()))
TN = (((0,), (0,)), ((), ()))


def _split2(a):
    hi = a.astype(BF16)
    lo = (a - hi.astype(F32)).astype(BF16)
    return hi, lo


def _mm(a, b, dims=NN, passes=1):
    if passes == 1:
        return lax.dot_general(a.astype(BF16), b.astype(BF16), dims, preferred_element_type=F32)
    a1, a2 = _split2(a)
    b1, b2 = _split2(b)
    out = lax.dot_general(a1, b1, dims, preferred_element_type=F32)
    out = out + lax.dot_general(a1, b2, dims, preferred_element_type=F32)
    return out + lax.dot_general(a2, b1, dims, preferred_element_type=F32)


def _mm_exact_lhs(a_bf, b):
    b1 = b.astype(BF16)
    r = b - b1.astype(F32)
    b2 = r.astype(BF16)
    b3 = (r - b2.astype(F32)).astype(BF16)
    out = jnp.dot(a_bf, b1, preferred_element_type=F32)
    out = out + jnp.dot(a_bf, b2, preferred_element_type=F32)
    return out + jnp.dot(a_bf, b3, preferred_element_type=F32)


def _sigmoid(x):
    return 1.0 / (1.0 + jnp.exp(-x))


def _silu(x):
    return x * _sigmoid(x)


def _softplus(x):
    return jnp.maximum(x, 0.0) + jnp.log1p(jnp.exp(-jnp.abs(x)))


def _tri_incl_bf16(n):
    r = lax.broadcasted_iota(jnp.int32, (n, n), 0)
    c = lax.broadcasted_iota(jnp.int32, (n, n), 1)
    return (r >= c).astype(BF16)


def _params(sem):
    return pltpu.CompilerParams(dimension_semantics=sem, vmem_limit_bytes=VMEM_LIMIT)


def _ada_kernel(c_ref, w_ref, b_ref, o_ref):
    cs = _silu(c_ref[...])
    o_ref[...] = _mm(cs, w_ref[...]) + b_ref[...]


def _ada_mod(c_all, w_ada, b_ada):
    depth, d, n = w_ada.shape
    rows = c_all.shape[0]
    tn = 1536
    return pl.pallas_call(
        _ada_kernel,
        grid=(depth, n // tn),
        in_specs=[
            pl.BlockSpec((rows, d), lambda l, j: (0, 0)),
            pl.BlockSpec((None, d, tn), lambda l, j: (l, 0, j)),
            pl.BlockSpec((None, 1, tn), lambda l, j: (l, 0, j)),
        ],
        out_specs=pl.BlockSpec((None, rows, tn), lambda l, j: (l, 0, j)),
        out_shape=jax.ShapeDtypeStruct((depth, rows, n), F32),
        compiler_params=_params(("parallel", "parallel")),
        name="ada_mod",
    )(c_all, w_ada, b_ada.reshape(depth, 1, n))


def _premix_kernel(has_vres, *refs):
    if has_vres:
        (x_ref, bvec_ref, vecd_ref, vec5_ref, mup_ref, win_ref, w1_ref, a1_ref, g1_ref, v1_ref,
         w2_ref, a2_ref, g2_ref, v2_ref, seg_ref, vfirst_ref,
         hg_ref, rw_ref, hlast_ref, hcar, pcar) = refs
    else:
        (x_ref, bvec_ref, vecd_ref, vec5_ref, mup_ref, win_ref, w1_ref, a1_ref, g1_ref,
         w2_ref, a2_ref, g2_ref, seg_ref,
         hg_ref, rw_ref, hlast_ref, hcar, pcar) = refs
    t = pl.program_id(1)
    tm = x_ref.shape[0]
    w_rw = win_ref[:, HG_PROJ:]

    @pl.when(t == 0)
    def _():
        hp = jnp.broadcast_to(bvec_ref[2:3, :], (8, D_MODEL))
        hcar[...] = hp
        pcar[...] = _mm(hp, w_rw)

    x = x_ref[...]
    nw = vecd_ref[0:1, :]
    sh = bvec_ref[0:1, :]
    sc = bvec_ref[1:2, :]
    xn = x * lax.rsqrt(jnp.mean(x * x, axis=-1, keepdims=True) + NORM_EPS) * nw
    h = xn * (1.0 + sc) + sh

    hb = h.astype(BF16)
    hg_ref[...] = jnp.dot(hb, win_ref[:, :HG_PROJ], preferred_element_type=F32)
    p_cur = jnp.dot(hb, w_rw, preferred_element_type=F32)

    first = lax.broadcasted_iota(jnp.int32, (tm, 1), 0) == 0
    h_shift = jnp.where(first, hcar[0:1, :], pltpu.roll(h, 1, 0))
    p_shift = jnp.where(first, pcar[0:1, :], pltpu.roll(p_cur, 1, 0))
    hcar[0:1, :] = h[tm - 1:tm, :]
    pcar[0:1, :] = p_cur[tm - 1:tm, :]
    hlast_ref[...] = h[tm - 1:tm, :]

    pm = p_cur + (p_shift - p_cur) * mup_ref[...]
    r = pm[:, :RW_WIDTH]
    k = pm[:, RW_WIDTH:2 * RW_WIDTH]
    v = pm[:, 2 * RW_WIDTH:]

    dh = h_shift - h
    xw = h + dh * vecd_ref[1:2, :]
    xa = h + dh * vecd_ref[2:3, :]
    xg = h + dh * vecd_ref[3:4, :]
    w_pre = vec5_ref[0:1, :] + _mm(jnp.tanh(_mm(xw, w1_ref[...])), w2_ref[...])
    logw = -jnp.exp(-_softplus(-w_pre) - 0.5)
    a = _sigmoid(vec5_ref[1:2, :] + _mm(_mm(xa, a1_ref[...]), a2_ref[...]))
    g = _mm(_sigmoid(_mm(xg, g1_ref[...])), g2_ref[...])
    if has_vres:
        xv = h + dh * vecd_ref[4:5, :]
        nu = _sigmoid(vec5_ref[2:3, :] + _mm(_mm(xv, v1_ref[...]), v2_ref[...]))
        v = v + (vfirst_ref[...] - v) * nu

    kk = k * vec5_ref[3:4, :]
    k2hi, k2lo = _split2(kk * kk)
    ss = (jnp.dot(k2hi, seg_ref[...], preferred_element_type=F32)
          + jnp.dot(k2lo, seg_ref[...], preferred_element_type=F32))
    kappa = kk / jnp.maximum(jnp.sqrt(ss), 1e-12)
    k_mod = k * (1.0 + (a - 1.0) * vec5_ref[4:5, :])

    rw_ref[:, 0 * RW_WIDTH:1 * RW_WIDTH] = r
    rw_ref[:, 1 * RW_WIDTH:2 * RW_WIDTH] = logw
    rw_ref[:, 2 * RW_WIDTH:3 * RW_WIDTH] = k_mod
    rw_ref[:, 3 * RW_WIDTH:4 * RW_WIDTH] = v
    rw_ref[:, 4 * RW_WIDTH:5 * RW_WIDTH] = kappa
    rw_ref[:, 5 * RW_WIDTH:6 * RW_WIDTH] = kappa * a
    rw_ref[:, 6 * RW_WIDTH:7 * RW_WIDTH] = g


def _premix(x, bvec, vecd, vec5, mup, win, loras, seg, vfirst, tm):
    b, l, d = x.shape
    has_vres = vfirst is not None
    const = lambda shape: pl.BlockSpec(shape, lambda i, j: tuple(0 for _ in shape),
                                       pipeline_mode=pl.Buffered(1))
    in_specs = [
        pl.BlockSpec((None, tm, d), lambda i, j: (i, j, 0)),
        pl.BlockSpec((None, 8, d), lambda i, j: (i, 0, 0)),
        const(vecd.shape), const(vec5.shape), const(mup.shape), const(win.shape),
    ]
    in_specs += [const(w.shape) for w in loras]
    in_specs.append(const(seg.shape))
    args = [x, bvec, vecd, vec5, mup, win, *loras, seg]
    if has_vres:
        in_specs.append(pl.BlockSpec((None, tm, RW_WIDTH), lambda i, j: (i, j, 3)))
        args.append(vfirst)
    return pl.pallas_call(
        functools.partial(_premix_kernel, has_vres),
        grid=(b, l // tm),
        in_specs=in_specs,
        out_specs=[
            pl.BlockSpec((None, tm, HG_PROJ), lambda i, j: (i, j, 0)),
            pl.BlockSpec((None, tm, RW_PACK), lambda i, j: (i, j, 0)),
            pl.BlockSpec((None, 1, d), lambda i, j: (i, 0, 0)),
        ],
        out_shape=[
            jax.ShapeDtypeStruct((b, l, HG_PROJ), F32),
            jax.ShapeDtypeStruct((b, l, RW_PACK), F32),
            jax.ShapeDtypeStruct((b, 1, d), F32),
        ],
        scratch_shapes=[pltpu.VMEM((8, d), F32), pltpu.VMEM((8, RW_PROJ), F32)],
        compiler_params=_params(("parallel", "arbitrary")),
        name="premix",
    )(*args)


def _hgrn_chunk(zq, zf, zi, zg, st, loglb, log1mlb, nw, tri):
    c = zq.shape[0]
    q = _silu(zq)
    bterm = log1mlb + (jnp.minimum(zf, 0.0) - jnp.log1p(jnp.exp(-jnp.abs(zf))))
    logf = jnp.maximum(loglb, bterm) + jnp.log1p(jnp.exp(-jnp.abs(loglb - bterm)))
    k = 1.0 - jnp.exp(logf)
    v = zi
    bcum = _mm_exact_lhs(tri, logf) * LOG2E
    row = lax.broadcasted_iota(jnp.int32, (c, 1), 0)
    rr = lax.broadcasted_iota(jnp.int32, (c, c), 0)
    cc = lax.broadcasted_iota(jnp.int32, (c, c), 1)
    a = jnp.zeros((c, c), F32)
    m = c // 2
    while m >= HG_SUB:
        bmid = jnp.concatenate([jnp.broadcast_to(bcum[(2 * j + 1) * m - 1:(2 * j + 1) * m], (2 * m, HG_DK))
                                for j in range(c // (2 * m))], axis=0)
        upper = ((row // m) % 2) == 1
        qt = jnp.where(upper, q * jnp.exp2(bcum - bmid), 0.0)
        kt = jnp.where(upper, 0.0, k * jnp.exp2(bmid - bcum))
        a = a + jnp.where((rr // (2 * m)) == (cc // (2 * m)), _mm(qt, kt, NT), 0.0)
        m //= 2
    lane = lax.broadcasted_iota(jnp.int32, (HG_SUB, c), 1)
    rws = lax.broadcasted_iota(jnp.int32, (HG_SUB, c), 0)
    diag = []
    for i in range(c // HG_SUB):
        lo = i * HG_SUB
        bi = bcum[lo:lo + HG_SUB]
        qi = q[lo:lo + HG_SUB]
        ki = k[lo:lo + HG_SUB]
        d = jnp.zeros((HG_SUB, c), F32)
        for s in range(HG_SUB):
            dec = jnp.exp2(bi - bi[s:s + 1])
            col = jnp.sum(qi * ki[s:s + 1] * dec, axis=-1, keepdims=True)
            d = jnp.where((lane == lo + s) & (rws >= s), col, d)
        diag.append(d)
    a = a + jnp.concatenate(diag, axis=0)
    o = _mm(q * jnp.exp2(bcum), st, NT) + _mm(a, v)
    b_last = bcum[c - 1:c]
    st_new = st * jnp.exp2(b_last) + _mm(v, k * jnp.exp2(b_last - bcum), TN)
    on = o * lax.rsqrt(jnp.mean(o * o, axis=-1, keepdims=True) + NORM_EPS) * nw
    return on * _silu(zg), st_new


def _hgrn_steps(c, hg_ref, vec_ref, o_ref, sfin_ref):
    tri = _tri_incl_bf16(c)
    qk = HG_HEADS * HG_DK

    def step(ci):
        r0 = ci * c
        for h in range(HG_HEADS):
            zq = hg_ref[r0:r0 + c, h * HG_DK:(h + 1) * HG_DK]
            zf = hg_ref[r0:r0 + c, qk + h * HG_DK:qk + (h + 1) * HG_DK]
            zi = hg_ref[r0:r0 + c, 2 * qk + h * HG_DV:2 * qk + (h + 1) * HG_DV]
            zg = hg_ref[r0:r0 + c, 2 * qk + HG_WIDTH + h * HG_DV:2 * qk + HG_WIDTH + (h + 1) * HG_DV]
            sl = slice(h * HG_DK, (h + 1) * HG_DK)
            out, st_new = _hgrn_chunk(zq, zf, zi, zg, sfin_ref[h], vec_ref[0:1, sl], vec_ref[1:2, sl],
                                      vec_ref[2:3, h * HG_DV:(h + 1) * HG_DV], tri)
            sfin_ref[h] = st_new
            o_ref[r0:r0 + c, h * HG_DV:(h + 1) * HG_DV] = out

    return step


RW_PASSES_A = 1
RW_PASSES_B = 1
RW_GROUP = 2


def _rwkv_chunk_matrices(c, chunks, tri, strict, incl, same_sub):
    pa = RW_PASSES_A
    prep = []
    for r, lw, km, v, kap, beta in chunks:
        cum = _mm_exact_lhs(tri, lw)
        c_last = cum[c - 1:c]
        e_neg = jnp.exp(-cum)
        e_hat = jnp.exp(c_last - cum)
        prep.append(dict(kbar=kap * jnp.exp(cum - lw), rbar=r * jnp.exp(cum), bneg=beta * e_neg,
                         kneg=km * e_neg, bhat=beta * e_hat, khat=km * e_hat, v=v, e_last=jnp.exp(c_last)))
    e_lasts = [pr['e_last'] for pr in prep]
    streams = [(ci, h) for ci in range(len(chunks)) for h in range(RW_HEADS)]
    hs = range(len(streams))

    def part(name):
        return [prep[ci][name][:, h * RW_HEAD:(h + 1) * RW_HEAD] for ci, h in streams]

    kbar, rbar, bneg, kneg, bhat, khat, vh = (part(n) for n in ('kbar', 'rbar', 'bneg', 'kneg', 'bhat', 'khat', 'v'))
    mask_kq = jnp.concatenate([strict, incl], axis=0)
    lhs = [jnp.concatenate([kbar[h], rbar[h]], axis=0) for h in hs]
    rhs = [jnp.concatenate([bneg[h], kneg[h]], axis=0) for h in hs]
    gmat = [_mm(lhs[h], rhs[h], NT, passes=pa) for h in hs]
    aab = [jnp.where(strict, gmat[h][:c, :c], 0.0) for h in hs]
    aqb = [jnp.where(incl, gmat[h][c:, :c], 0.0) for h in hs]
    akq = [jnp.where(mask_kq, gmat[h][:, c:], 0.0) for h in hs]
    av = [_mm(akq[h], vh[h], passes=pa) for h in hs]
    d1 = [jnp.where(same_sub, aab[h], 0.0) for h in hs]
    if c == SUB:
        z = [jnp.concatenate([kbar[h], av[h][:c]], axis=1) for h in hs]
    else:
        z = [jnp.concatenate([aab[h] - d1[h], kbar[h], av[h][:c]], axis=1) for h in hs]
    d2 = [_mm(d1[h], d1[h], passes=pa) for h in hs]
    z = [z[h] - _mm(d1[h], z[h], passes=pa) for h in hs]
    d4 = [_mm(d2[h], d2[h], passes=pa) for h in hs]
    z = [z[h] + _mm(d2[h], z[h], passes=pa) for h in hs]
    d8 = [_mm(d4[h], d4[h], passes=pa) for h in hs]
    z = [z[h] + _mm(d4[h], z[h], passes=pa) for h in hs]
    z = [z[h] + _mm(d8[h], z[h], passes=pa) for h in hs]
    if c == SUB:
        x = z
    else:
        f = [z[h][:, :c] for h in hs]
        x = [z[h][:, c:] for h in hs]
        fx = [_mm(f[h], x[h], passes=pa) for h in hs]
        x = [x[h] + _mm(f[h], fx[h], passes=pa) for h in hs]
        x = [x[h] - _mm(f[h], x[h], passes=pa) for h in hs]
    qb = [_mm(aqb[h], x[h], passes=pa) for h in hs]
    rt = [rbar[h] - qb[h][:, :RW_HEAD] for h in hs]
    yc = [av[h][c:] - qb[h][:, RW_HEAD:] for h in hs]
    ktb = [_mm(x[h][:, :RW_HEAD], bhat[h], TN, passes=pa) for h in hs]
    nc = [_mm(jnp.concatenate([vh[h], x[h][:, RW_HEAD:]], axis=0),
              jnp.concatenate([khat[h], -bhat[h]], axis=0), TN, passes=pa) for h in hs]
    return rt, yc, ktb, nc, e_lasts


def _rwkv_steps(c, rw_ref, vec_ref, o_ref, sfin_ref):
    tri = _tri_incl_bf16(c)
    row = lax.broadcasted_iota(jnp.int32, (c, c), 0)
    col = lax.broadcasted_iota(jnp.int32, (c, c), 1)
    strict = row > col
    incl = row >= col
    same_sub = (row // SUB) == (col // SUB)
    w = RW_WIDTH
    hs = range(RW_HEADS)
    sl = [slice(h * RW_HEAD, (h + 1) * RW_HEAD) for h in hs]
    state = [sfin_ref[h] for h in hs]
    n_chunks = rw_ref.shape[0] // c

    def step(g0):
        group = range(g0, min(g0 + RW_GROUP, n_chunks))
        chunks = [tuple(rw_ref[ci * c:(ci + 1) * c, j * w:(j + 1) * w] for j in range(6)) for ci in group]
        rt, yc, ktb, nc, e_lasts = _rwkv_chunk_matrices(c, chunks, tri, strict, incl, same_sub)
        for gi, ci in enumerate(group):
            r0 = ci * c
            r, _, km, v, _, _ = chunks[gi]
            gate = rw_ref[r0:r0 + c, 6 * w:7 * w]
            q = gi * RW_HEADS
            y = [_mm(rt[q + h], state[h], NT, passes=RW_PASSES_B) + yc[q + h] for h in hs]
            state[:] = [state[h] * e_lasts[gi][:, sl[h]] - _mm(state[h], ktb[q + h], passes=RW_PASSES_B)
                        + nc[q + h] for h in hs]
            bonus = r * km * vec_ref[2:3, :]
            for h in hs:
                mean = jnp.mean(y[h], axis=-1, keepdims=True)
                yd = y[h] - mean
                var = jnp.mean(yd * yd, axis=-1, keepdims=True)
                yn = yd * lax.rsqrt(var + RW_GN_EPS) * vec_ref[0:1, sl[h]] + vec_ref[1:2, sl[h]]
                yn = yn + jnp.sum(bonus[:, sl[h]], axis=-1, keepdims=True) * v[:, sl[h]]
                o_ref[r0:r0 + c, sl[h]] = yn * gate[:, sl[h]]

    def finish():
        for h in hs:
            sfin_ref[h] = state[h]

    return step, finish


def _hgrn_kernel(c, hg_ref, vec_ref, s0_ref, o_ref, sfin_ref):
    @pl.when(pl.program_id(1) == 0)
    def _():
        sfin_ref[...] = s0_ref[...]

    step = _hgrn_steps(c, hg_ref, vec_ref, o_ref, sfin_ref)
    for ci in range(hg_ref.shape[0] // c):
        step(ci)


def _rwkv_kernel(c, rw_ref, vec_ref, s0_ref, o_ref, sfin_ref):
    @pl.when(pl.program_id(1) == 0)
    def _():
        sfin_ref[...] = s0_ref[...]

    step, finish = _rwkv_steps(c, rw_ref, vec_ref, o_ref, sfin_ref)
    for g0 in range(0, rw_ref.shape[0] // c, RW_GROUP):
        step(g0)
    finish()


def _recurrence(kernel, name, x, vec, s0, width, ch, c):
    b, l, xw = x.shape
    state = pl.BlockSpec((None,) + s0.shape[1:], lambda i, j: (i, 0, 0, 0))
    return pl.pallas_call(
        functools.partial(kernel, c),
        grid=(b, l // ch),
        in_specs=[
            pl.BlockSpec((None, ch, xw), lambda i, j: (i, j, 0)),
            pl.BlockSpec(vec.shape, lambda i, j: (0, 0)),
            state,
        ],
        out_specs=[pl.BlockSpec((None, ch, width), lambda i, j: (i, j, 0)), state],
        out_shape=[jax.ShapeDtypeStruct((b, l, width), F32), jax.ShapeDtypeStruct(s0.shape, F32)],
        compiler_params=_params(("parallel", "arbitrary")),
        name=name,
    )(x, vec, s0)


def _postmix_kernel(ohg_ref, yrw_ref, x_ref, bvec_ref, nw_ref, wout_ref, wr_ref, br_ref,
                    xn_ref, h2_ref, route_ref):
    m = (jnp.dot(ohg_ref[...].astype(BF16), wout_ref[:HG_WIDTH, :], preferred_element_type=F32)
         + jnp.dot(yrw_ref[...].astype(BF16), wout_ref[HG_WIDTH:, :], preferred_element_type=F32))
    xn = x_ref[...] + bvec_ref[0:1, :] * m
    xn_ref[...] = xn
    h2 = (xn * lax.rsqrt(jnp.mean(xn * xn, axis=-1, keepdims=True) + NORM_EPS) * nw_ref[...]
          * (1.0 + bvec_ref[2:3, :]) + bvec_ref[1:2, :])
    h2_ref[...] = h2.astype(BF16)

    logits = _mm(h2, wr_ref[...]) + br_ref[...]
    lane = lax.broadcasted_iota(jnp.int32, logits.shape, 1)
    neg = jnp.float32(-jnp.inf)
    big = jnp.int32(ROUTE_W)
    is_g = lane < N_GROUPS
    gl = jnp.where(is_g, logits, neg)
    gmax = jnp.max(gl, axis=-1, keepdims=True)
    grp = jnp.min(jnp.where(gl == gmax, lane, big), axis=-1, keepdims=True)
    p_grp = 1.0 / jnp.sum(jnp.where(is_g, jnp.exp(gl - gmax), 0.0), axis=-1, keepdims=True)
    e_lo = N_GROUPS + grp * EXPERTS_PER_GROUP
    in_grp = (lane >= e_lo) & (lane < e_lo + EXPERTS_PER_GROUP)
    el = jnp.where(in_grp, logits, neg)
    m1 = jnp.max(el, axis=-1, keepdims=True)
    i1 = jnp.min(jnp.where(el == m1, lane, big), axis=-1, keepdims=True)
    el2 = jnp.where(lane == i1, neg, el)
    m2 = jnp.max(el2, axis=-1, keepdims=True)
    i2 = jnp.min(jnp.where(el2 == m2, lane, big), axis=-1, keepdims=True)
    e21 = jnp.exp(m2 - m1)
    p1 = 1.0 / (1.0 + e21)
    p2 = e21 / (1.0 + e21)
    out = jnp.where(lane == 0, (i1 - N_GROUPS).astype(F32),
                    jnp.where(lane == 1, (i2 - N_GROUPS).astype(F32),
                              jnp.where(lane == 2, p_grp * p1,
                                        jnp.where(lane == 3, p_grp * p2, 0.0))))
    route_ref[...] = out[:, :8]


def _postmix(ohg, yrw, x, bvec, nw, wout, wr, br, tm):
    b, l, d = x.shape
    const = lambda shape: pl.BlockSpec(shape, lambda i, j: tuple(0 for _ in shape))
    return pl.pallas_call(
        _postmix_kernel,
        grid=(b, l // tm),
        in_specs=[
            pl.BlockSpec((None, tm, HG_WIDTH), lambda i, j: (i, j, 0)),
            pl.BlockSpec((None, tm, RW_WIDTH), lambda i, j: (i, j, 0)),
            pl.BlockSpec((None, tm, d), lambda i, j: (i, j, 0)),
            pl.BlockSpec((None, 8, d), lambda i, j: (i, 0, 0)),
            const(nw.shape), const(wout.shape), const(wr.shape), const(br.shape),
        ],
        out_specs=[
            pl.BlockSpec((None, tm, d), lambda i, j: (i, j, 0)),
            pl.BlockSpec((None, tm, d), lambda i, j: (i, j, 0)),
            pl.BlockSpec((None, tm, 8), lambda i, j: (i, j, 0)),
        ],
        out_shape=[
            jax.ShapeDtypeStruct((b, l, d), F32),
            jax.ShapeDtypeStruct((b, l, d), BF16),
            jax.ShapeDtypeStruct((b, l, 8), F32),
        ],
        compiler_params=_params(("parallel", "parallel")),
        name="postmix",
    )(ohg, yrw, x, bvec, nw, wout, wr, br)


def _expert_kernel(be_ref, nu_ref, xs_ref, wg_ref, wu_ref, wd_ref, o_ref):
    i = pl.program_id(0)

    @pl.when(i < nu_ref[0])
    def _():
        xb = xs_ref[...].astype(BF16)
        hg = jnp.dot(xb, wg_ref[...].astype(BF16), preferred_element_type=F32)
        hu = jnp.dot(xb, wu_ref[...].astype(BF16), preferred_element_type=F32)
        hdn = (_silu(hg) * hu).astype(BF16)
        o_ref[...] = jnp.dot(hdn, wd_ref[...].astype(BF16), preferred_element_type=F32)

    @pl.when(i >= nu_ref[0])
    def _():
        o_ref[...] = jnp.zeros_like(o_ref)


def _experts(xs, block_e, n_used, wg, wu, wd, layer, blk):
    p, d = xs.shape
    nb = p // blk
    grid_spec = pltpu.PrefetchScalarGridSpec(
        num_scalar_prefetch=2,
        grid=(nb,),
        in_specs=[
            pl.BlockSpec((blk, d), lambda i, be, nu: (i, 0)),
            pl.BlockSpec((None, None, d, D_EXPERT), lambda i, be, nu: (layer, be[i], 0, 0)),
            pl.BlockSpec((None, None, d, D_EXPERT), lambda i, be, nu: (layer, be[i], 0, 0)),
            pl.BlockSpec((None, None, D_EXPERT, d), lambda i, be, nu: (layer, be[i], 0, 0)),
        ],
        out_specs=pl.BlockSpec((blk, d), lambda i, be, nu: (i, 0)),
    )
    return pl.pallas_call(
        _expert_kernel,
        grid_spec=grid_spec,
        out_shape=jax.ShapeDtypeStruct((p, d), F32),
        compiler_params=_params(("arbitrary",)),
        name="experts",
    )(block_e, n_used, xs, wg, wu, wd)


def _combine_kernel(final, xn_ref, y0_ref, y1_ref, route_ref, bvec_ref, fw_ref, o_ref):
    rt = route_ref[...]
    moe = rt[:, 2:3] * y0_ref[...] + rt[:, 3:4] * y1_ref[...]
    x = xn_ref[...] + bvec_ref[3:4, :] * moe
    if final:
        x = x * lax.rsqrt(jnp.mean(x * x, axis=-1, keepdims=True) + NORM_EPS) * fw_ref[...]
    o_ref[...] = x


def _combine(xn, y0, y1, route, bvec, fw, final, tm):
    b, l, d = xn.shape
    row = lambda w: pl.BlockSpec((None, tm, w), lambda i, j: (i, j, 0))
    return pl.pallas_call(
        functools.partial(_combine_kernel, final),
        grid=(b, l // tm),
        in_specs=[row(d), row(d), row(d), row(8),
                  pl.BlockSpec((None, 8, d), lambda i, j: (i, 0, 0)),
                  pl.BlockSpec((1, d), lambda i, j: (0, 0))],
        out_specs=row(d),
        out_shape=jax.ShapeDtypeStruct((b, l, d), F32),
        compiler_params=_params(("parallel", "parallel")),
        name="combine",
    )(xn, y0, y1, route, bvec, fw)


def _dispatch(route, blk):
    t = route.shape[0]
    n = t * TOP_K
    flat_e = route[:, :TOP_K].astype(jnp.int32).reshape(n)
    experts = jnp.arange(N_EXPERTS, dtype=jnp.int32)
    onehot = (flat_e[:, None] == experts[None, :]).astype(jnp.int32)
    csum = jnp.cumsum(onehot, axis=0)
    counts = csum[-1]
    padded = ((counts + blk - 1) // blk) * blk
    start = jnp.cumsum(counts) - counts
    pend = jnp.cumsum(padded)
    pstart = pend - padded
    slot = jnp.sum(onehot * (csum - 1 + pstart[None, :]), axis=1)
    nb = -(-(n + N_EXPERTS * (blk - 1)) // blk)
    block_e = jnp.minimum(jnp.sum((pend // blk)[None, :] <= jnp.arange(nb, dtype=jnp.int32)[:, None], axis=1),
                          N_EXPERTS - 1).astype(jnp.int32)
    n_used = (pend[-1] // blk).astype(jnp.int32).reshape(1)
    order = jnp.argsort(flat_e).astype(jnp.int32)
    row_e = jnp.repeat(block_e, blk)
    row = jnp.arange(nb * blk, dtype=jnp.int32)
    off = row - pstart[row_e]
    src = jnp.clip(start[row_e] + off, 0, n - 1)
    buf_tok = jnp.where(off < counts[row_e], order[src] // TOP_K, row % t)
    return buf_tok, slot.reshape(t, TOP_K), block_e, n_used


def _layer_weights(p, l):
    f = lambda name: p[name][l]
    zero_d = jnp.zeros((D_MODEL,), F32)
    has_vres = l > 0
    vecd = jnp.stack([f('norm_mix_w'), p['rw_mu_lora'][l, 0], p['rw_mu_lora'][l, 1], p['rw_mu_lora'][l, 2],
                      p['rw_mu_vres'][l - 1] if has_vres else zero_d, zero_d, zero_d, zero_d])
    zero_w = jnp.zeros((RW_WIDTH,), F32)
    vec5 = jnp.stack([f('rw_w0'), f('rw_a0'), p['rw_v0'][l - 1] if has_vres else zero_w,
                      f('rw_kk'), f('rw_ka'), zero_w, zero_w, zero_w])
    loras = [f('rw_w1'), f('rw_a1'), f('rw_g1')] + ([p['rw_v1'][l - 1]] if has_vres else [])
    loras += [f('rw_w2'), f('rw_a2'), f('rw_g2')] + ([p['rw_v2'][l - 1]] if has_vres else [])
    loras = [w.astype(BF16) for w in loras]
    rwvec = jnp.stack([f('rw_ln_w'), f('rw_ln_b'), f('rw_rk').reshape(RW_WIDTH),
                       zero_w, zero_w, zero_w, zero_w, zero_w])
    wr = jnp.concatenate([f('router_g_w'),
                          jnp.transpose(f('router_e_w'), (1, 0, 2)).reshape(D_MODEL, N_EXPERTS),
                          jnp.zeros((D_MODEL, ROUTE_W - N_GROUPS - N_EXPERTS), F32)], axis=1)
    br = jnp.concatenate([f('router_g_b'), f('router_e_b').reshape(N_EXPERTS),
                          jnp.zeros((ROUTE_W - N_GROUPS - N_EXPERTS,), F32)]).reshape(1, ROUTE_W)
    return dict(
        vecd=vecd, vec5=vec5, mup=f('rw_mu_proj').reshape(1, RW_PROJ), win=f('w_in').astype(BF16),
        loras=loras, rwvec=rwvec, hg_nw=f('hg_norm_w').reshape(1, HG_WIDTH),
        nfw=f('norm_ffn_w').reshape(1, D_MODEL), wout=f('w_out').astype(BF16), wr=wr, br=br)


def _trunk(x, mod, s_hg, s_rw, h_prev, lw_all, expert_w, lbvecs, final_w):
    b, l, d = x.shape
    depth = len(lw_all)
    tm = min(l, 512)
    ch = min(l, 256)
    c = min(l, CHUNK)
    blk = 512 if b * l * TOP_K >= 8192 else 32
    head = lax.broadcasted_iota(jnp.int32, (RW_WIDTH, RW_WIDTH), 0) // RW_HEAD
    seg = (head == head.T).astype(BF16)
    new_hg, new_rw, new_shift = [], [], []
    rwp_first = None
    zrow = jnp.zeros((b, d), F32)
    for li in range(depth):
        w = lw_all[li]
        sh1, sc1, g1, sh2, sc2, g2 = jnp.split(mod[li], 6, axis=-1)
        bvec1 = jnp.stack([sh1, sc1, h_prev[li], zrow, zrow, zrow, zrow, zrow], axis=1)
        bvec2 = jnp.stack([g1, sh2, sc2, g2, zrow, zrow, zrow, zrow], axis=1)
        hgraw, rwp, hlast = _premix(x, bvec1, w['vecd'], w['vec5'], w['mup'], w['win'], w['loras'],
                                    seg, rwp_first if li > 0 else None, min(l, 512))
        if li == 0:
            rwp_first = rwp
        hgvec = jnp.concatenate([lbvecs[li][:2], w['hg_nw'], lbvecs[li][3:]], axis=0)
        ohg, hg_t = _recurrence(_hgrn_kernel, "hgrn", hgraw, hgvec, jnp.swapaxes(s_hg[li], -1, -2),
                                HG_WIDTH, ch, c)
        yrw, rw_fin = _recurrence(_rwkv_kernel, "rwkv", rwp, w['rwvec'], s_rw[li], RW_WIDTH, ch, c)
        xn, h2, route = _postmix(ohg, yrw, x, bvec2, w['nfw'], w['wout'], w['wr'], w['br'], tm)
        t = b * l
        buf_tok, slot, block_e, n_used = _dispatch(route.reshape(t, 8), blk)
        xs = h2.reshape(t, d).at[buf_tok].get(mode='promise_in_bounds')
        yb = _experts(xs, block_e, n_used, *expert_w, li, blk)
        y0 = yb.at[slot[:, 0]].get(mode='promise_in_bounds').reshape(b, l, d)
        y1 = yb.at[slot[:, 1]].get(mode='promise_in_bounds').reshape(b, l, d)
        x = _combine(xn, y0, y1, route, bvec2, final_w, li == depth - 1, tm)
        new_hg.append(jnp.swapaxes(hg_t, -1, -2))
        new_rw.append(rw_fin)
        new_shift.append(hlast.reshape(b, d))
    return x, jnp.stack(new_hg), jnp.stack(new_rw), jnp.stack(new_shift)


def kernel(x_prompt, x_sample, state_hgrn, state_rwkv, state_shift, c_prompt, c_sample, w_ada, b_ada, norm_mix_w, norm_ffn_w, w_in, w_out, hg_lb_logits, hg_norm_w, rw_mu_proj, rw_mu_lora, rw_w0, rw_w1, rw_w2, rw_a0, rw_a1, rw_a2, rw_g1, rw_g2, rw_mu_vres, rw_v0, rw_v1, rw_v2, rw_kk, rw_ka, rw_rk, rw_ln_w, rw_ln_b, router_g_w, router_g_b, router_e_w, router_e_b, w_gate, w_up, w_down, final_norm_w):
    p = dict(norm_mix_w=norm_mix_w, norm_ffn_w=norm_ffn_w, w_in=w_in, w_out=w_out, hg_norm_w=hg_norm_w,
             rw_mu_proj=rw_mu_proj, rw_mu_lora=rw_mu_lora, rw_w0=rw_w0, rw_w1=rw_w1, rw_w2=rw_w2,
             rw_a0=rw_a0, rw_a1=rw_a1, rw_a2=rw_a2, rw_g1=rw_g1, rw_g2=rw_g2,
             rw_mu_vres=rw_mu_vres, rw_v0=rw_v0, rw_v1=rw_v1, rw_v2=rw_v2,
             rw_kk=rw_kk, rw_ka=rw_ka, rw_rk=rw_rk, rw_ln_w=rw_ln_w, rw_ln_b=rw_ln_b,
             router_g_w=router_g_w, router_g_b=router_g_b, router_e_w=router_e_w,
             router_e_b=router_e_b, w_gate=w_gate, w_up=w_up, w_down=w_down)
    depth = w_in.shape[0]
    bp = x_prompt.shape[0]
    bs = x_sample.shape[0]
    d = x_prompt.shape[-1]
    lw_all = [_layer_weights(p, l) for l in range(depth)]
    lb = jnp.cumsum(jax.nn.softmax(hg_lb_logits.astype(F32), axis=0), axis=0)
    lb = lb - lb[0]
    zpad = jnp.zeros((6, lb.shape[1]), F32)
    lbvecs = [jnp.concatenate([jnp.log(lb[l])[None], jnp.log1p(-lb[l])[None], zpad], axis=0)
              for l in range(depth)]
    rows = -(-(bp + bs) // 8) * 8
    c_all = jnp.concatenate([c_prompt, c_sample, jnp.zeros((rows - bp - bs, d), F32)], axis=0)
    mod = _ada_mod(c_all, w_ada, b_ada)
    fw = final_norm_w.reshape(1, d)
    zero_hg = jnp.zeros((depth, bp) + state_hgrn.shape[2:], F32)
    zero_rw = jnp.zeros((depth, bp) + state_rwkv.shape[2:], F32)
    zero_sh = jnp.zeros((depth, bp, d), F32)
    expert_w = (w_gate, w_up, w_down)
    y_p, hg_p, rw_p, sh_p = _trunk(x_prompt, mod[:, :bp], zero_hg, zero_rw, zero_sh, lw_all, expert_w,
                                   lbvecs, fw)
    y_s, hg_s, rw_s, sh_s = _trunk(x_sample, mod[:, bp:bp + bs], state_hgrn, state_rwkv, state_shift,
                                   lw_all, expert_w, lbvecs, fw)
    return (y_p, y_s, hg_p, rw_p, sh_p, hg_s, rw_s, sh_s)
```

```python
import functools

import jax
import jax.numpy as jnp
from jax import lax
from jax.experimental import pallas as pl
from jax.experimental.pallas import tpu as pltpu

F32 = jnp.float32
BF16 = jnp.bfloat16

D_MODEL = 1024
HG_HEADS = 4
HG_DK = 128
HG_DV = 128
HG_WIDTH = HG_HEADS * HG_DV
RW_HEADS = 8
RW_HEAD = 64
RW_WIDTH = RW_HEADS * RW_HEAD
HG_PROJ = 2 * HG_HEADS * HG_DK + 2 * HG_WIDTH
RW_PROJ = 3 * RW_WIDTH
IN_WIDTH = HG_PROJ + RW_PROJ
N_GROUPS = 4
EXPERTS_PER_GROUP = 8
N_EXPERTS = N_GROUPS * EXPERTS_PER_GROUP
TOP_K = 2
D_EXPERT = 512
LOG2E = 1.4426950408889634
NORM_EPS = 1e-6
RW_GN_EPS = 64e-5

ROW_TILE = 512
SCAN_ROWS = 256
EXPERT_BLOCK = 512
EXPERT_BLOCK_SMALL = 32
ADA_COLS = 1536
CHUNK = 64
SUB = 16
HG_SUB = 8
ROUTE_W = 128
RW_PACK = 7 * RW_WIDTH
VMEM_LIMIT = 56 * 1024 * 1024

NN = (((1,), (0,)), ((), ()))
NT = (((1,), (1,)), ((), ()))
TN = (((0,), (0,)), ((), ()))


def _split2(a):
    hi = a.astype(BF16)
    lo = (a - hi.astype(F32)).astype(BF16)
    return hi, lo


def _mm(a, b, dims=NN, passes=1):
    if passes == 1:
        return lax.dot_general(a.astype(BF16), b.astype(BF16), dims, preferred_element_type=F32)
    a1, a2 = _split2(a)
    b1, b2 = _split2(b)
    out = lax.dot_general(a1, b1, dims, preferred_element_type=F32)
    out = out + lax.dot_general(a1, b2, dims, preferred_element_type=F32)
    return out + lax.dot_general(a2, b1, dims, preferred_element_type=F32)


def _mm_exact_lhs(a_bf, b):
    b1 = b.astype(BF16)
    r = b - b1.astype(F32)
    b2 = r.astype(BF16)
    b3 = (r - b2.astype(F32)).astype(BF16)
    out = jnp.dot(a_bf, b1, preferred_element_type=F32)
    out = out + jnp.dot(a_bf, b2, preferred_element_type=F32)
    return out + jnp.dot(a_bf, b3, preferred_element_type=F32)


def _sigmoid(x):
    return 1.0 / (1.0 + jnp.exp(-x))


def _silu(x):
    return x * _sigmoid(x)


def _softplus(x):
    return jnp.maximum(x, 0.0) + jnp.log1p(jnp.exp(-jnp.abs(x)))


def _tri_incl_bf16(n):
    r = lax.broadcasted_iota(jnp.int32, (n, n), 0)
    c = lax.broadcasted_iota(jnp.int32, (n, n), 1)
    return (r >= c).astype(BF16)


def _params(sem):
    return pltpu.CompilerParams(dimension_semantics=sem, vmem_limit_bytes=VMEM_LIMIT)


def _ada_kernel(c_ref, w_ref, b_ref, o_ref):
    cs = _silu(c_ref[...])
    o_ref[...] = _mm(cs, w_ref[...]) + b_ref[...]


def _ada_mod(c_all, w_ada, b_ada):
    depth, d, n = w_ada.shape
    rows = c_all.shape[0]
    tn = ADA_COLS
    return pl.pallas_call(
        _ada_kernel,
        grid=(depth, n // tn),
        in_specs=[
            pl.BlockSpec((rows, d), lambda l, j: (0, 0)),
            pl.BlockSpec((None, d, tn), lambda l, j: (l, 0, j)),
            pl.BlockSpec((None, 1, tn), lambda l, j: (l, 0, j)),
        ],
        out_specs=pl.BlockSpec((None, rows, tn), lambda l, j: (l, 0, j)),
        out_shape=jax.ShapeDtypeStruct((depth, rows, n), F32),
        compiler_params=_params(("parallel", "parallel")),
        name="ada_mod",
    )(c_all, w_ada, b_ada.reshape(depth, 1, n))


def _premix_kernel(has_vres, *refs):
    if has_vres:
        (x_ref, bvec_ref, vecd_ref, vec5_ref, mup_ref, win_ref, w1_ref, a1_ref, g1_ref, v1_ref,
         w2_ref, a2_ref, g2_ref, v2_ref, seg_ref, vfirst_ref,
         hg_ref, rw_ref, hlast_ref, hcar, pcar) = refs
    else:
        (x_ref, bvec_ref, vecd_ref, vec5_ref, mup_ref, win_ref, w1_ref, a1_ref, g1_ref,
         w2_ref, a2_ref, g2_ref, seg_ref,
         hg_ref, rw_ref, hlast_ref, hcar, pcar) = refs
    t = pl.program_id(1)
    tm = x_ref.shape[0]
    w_rw = win_ref[:, HG_PROJ:]

    @pl.when(t == 0)
    def _():
        hp = jnp.broadcast_to(bvec_ref[2:3, :], (8, D_MODEL))
        hcar[...] = hp
        pcar[...] = _mm(hp, w_rw)

    x = x_ref[...]
    nw = vecd_ref[0:1, :]
    sh = bvec_ref[0:1, :]
    sc = bvec_ref[1:2, :]
    xn = x * lax.rsqrt(jnp.mean(x * x, axis=-1, keepdims=True) + NORM_EPS) * nw
    h = xn * (1.0 + sc) + sh

    hb = h.astype(BF16)
    hg_ref[...] = jnp.dot(hb, win_ref[:, :HG_PROJ], preferred_element_type=F32)
    p_cur = jnp.dot(hb, w_rw, preferred_element_type=F32)

    first = lax.broadcasted_iota(jnp.int32, (tm, 1), 0) == 0
    h_shift = jnp.where(first, hcar[0:1, :], pltpu.roll(h, 1, 0))
    p_shift = jnp.where(first, pcar[0:1, :], pltpu.roll(p_cur, 1, 0))
    hcar[0:1, :] = h[tm - 1:tm, :]
    pcar[0:1, :] = p_cur[tm - 1:tm, :]
    hlast_ref[...] = h[tm - 1:tm, :]

    pm = p_cur + (p_shift - p_cur) * mup_ref[...]
    r = pm[:, :RW_WIDTH]
    k = pm[:, RW_WIDTH:2 * RW_WIDTH]
    v = pm[:, 2 * RW_WIDTH:]

    dh = h_shift - h
    xw = h + dh * vecd_ref[1:2, :]
    xa = h + dh * vecd_ref[2:3, :]
    xg = h + dh * vecd_ref[3:4, :]
    w_pre = vec5_ref[0:1, :] + _mm(jnp.tanh(_mm(xw, w1_ref[...])), w2_ref[...])
    logw = -jnp.exp(-_softplus(-w_pre) - 0.5)
    a = _sigmoid(vec5_ref[1:2, :] + _mm(_mm(xa, a1_ref[...]), a2_ref[...]))
    g = _mm(_sigmoid(_mm(xg, g1_ref[...])), g2_ref[...])
    if has_vres:
        xv = h + dh * vecd_ref[4:5, :]
        nu = _sigmoid(vec5_ref[2:3, :] + _mm(_mm(xv, v1_ref[...]), v2_ref[...]))
        v = v + (vfirst_ref[...] - v) * nu

    kk = k * vec5_ref[3:4, :]
    k2hi, k2lo = _split2(kk * kk)
    ss = (jnp.dot(k2hi, seg_ref[...], preferred_element_type=F32)
          + jnp.dot(k2lo, seg_ref[...], preferred_element_type=F32))
    kappa = kk / jnp.maximum(jnp.sqrt(ss), 1e-12)
    k_mod = k * (1.0 + (a - 1.0) * vec5_ref[4:5, :])

    rw_ref[:, 0 * RW_WIDTH:1 * RW_WIDTH] = r
    rw_ref[:, 1 * RW_WIDTH:2 * RW_WIDTH] = logw
    rw_ref[:, 2 * RW_WIDTH:3 * RW_WIDTH] = k_mod
    rw_ref[:, 3 * RW_WIDTH:4 * RW_WIDTH] = v
    rw_ref[:, 4 * RW_WIDTH:5 * RW_WIDTH] = kappa
    rw_ref[:, 5 * RW_WIDTH:6 * RW_WIDTH] = kappa * a
    rw_ref[:, 6 * RW_WIDTH:7 * RW_WIDTH] = g


def _premix(x, bvec, vecd, vec5, mup, win, loras, seg, vfirst, tm):
    b, l, d = x.shape
    has_vres = vfirst is not None
    const = lambda shape: pl.BlockSpec(shape, lambda i, j: tuple(0 for _ in shape),
                                       pipeline_mode=pl.Buffered(1))
    in_specs = [
        pl.BlockSpec((None, tm, d), lambda i, j: (i, j, 0)),
        pl.BlockSpec((None, 8, d), lambda i, j: (i, 0, 0)),
        const(vecd.shape), const(vec5.shape), const(mup.shape), const(win.shape),
    ]
    in_specs += [const(w.shape) for w in loras]
    in_specs.append(const(seg.shape))
    args = [x, bvec, vecd, vec5, mup, win, *loras, seg]
    if has_vres:
        in_specs.append(pl.BlockSpec((None, tm, RW_WIDTH), lambda i, j: (i, j, 3)))
        args.append(vfirst)
    return pl.pallas_call(
        functools.partial(_premix_kernel, has_vres),
        grid=(b, l // tm),
        in_specs=in_specs,
        out_specs=[
            pl.BlockSpec((None, tm, HG_PROJ), lambda i, j: (i, j, 0)),
            pl.BlockSpec((None, tm, RW_PACK), lambda i, j: (i, j, 0)),
            pl.BlockSpec((None, 1, d), lambda i, j: (i, 0, 0)),
        ],
        out_shape=[
            jax.ShapeDtypeStruct((b, l, HG_PROJ), F32),
            jax.ShapeDtypeStruct((b, l, RW_PACK), F32),
            jax.ShapeDtypeStruct((b, 1, d), F32),
        ],
        scratch_shapes=[pltpu.VMEM((8, d), F32), pltpu.VMEM((8, RW_PROJ), F32)],
        compiler_params=_params(("parallel", "arbitrary")),
        name="premix",
    )(*args)


def _hgrn_chunk(zq, zf, zi, zg, st, loglb, log1mlb, nw, tri):
    c = zq.shape[0]
    q = _silu(zq)
    bterm = log1mlb + (jnp.minimum(zf, 0.0) - jnp.log1p(jnp.exp(-jnp.abs(zf))))
    logf = jnp.maximum(loglb, bterm) + jnp.log1p(jnp.exp(-jnp.abs(loglb - bterm)))
    k = 1.0 - jnp.exp(logf)
    v = zi
    bcum = _mm_exact_lhs(tri, logf) * LOG2E
    row = lax.broadcasted_iota(jnp.int32, (c, 1), 0)
    rr = lax.broadcasted_iota(jnp.int32, (c, c), 0)
    cc = lax.broadcasted_iota(jnp.int32, (c, c), 1)
    a = jnp.zeros((c, c), F32)
    m = c // 2
    while m >= HG_SUB:
        bmid = jnp.concatenate([jnp.broadcast_to(bcum[(2 * j + 1) * m - 1:(2 * j + 1) * m], (2 * m, HG_DK))
                                for j in range(c // (2 * m))], axis=0)
        upper = ((row // m) % 2) == 1
        qt = jnp.where(upper, q * jnp.exp2(bcum - bmid), 0.0)
        kt = jnp.where(upper, 0.0, k * jnp.exp2(bmid - bcum))
        a = a + jnp.where((rr // (2 * m)) == (cc // (2 * m)), _mm(qt, kt, NT), 0.0)
        m //= 2
    lane = lax.broadcasted_iota(jnp.int32, (HG_SUB, c), 1)
    rws = lax.broadcasted_iota(jnp.int32, (HG_SUB, c), 0)
    diag = []
    for i in range(c // HG_SUB):
        lo = i * HG_SUB
        bi = bcum[lo:lo + HG_SUB]
        qi = q[lo:lo + HG_SUB]
        ki = k[lo:lo + HG_SUB]
        d = jnp.zeros((HG_SUB, c), F32)
        for s in range(HG_SUB):
            dec = jnp.exp2(bi - bi[s:s + 1])
            col = jnp.sum(qi * ki[s:s + 1] * dec, axis=-1, keepdims=True)
            d = jnp.where((lane == lo + s) & (rws >= s), col, d)
        diag.append(d)
    a = a + jnp.concatenate(diag, axis=0)
    o = _mm(q * jnp.exp2(bcum), st, NT) + _mm(a, v)
    b_last = bcum[c - 1:c]
    st_new = st * jnp.exp2(b_last) + _mm(v, k * jnp.exp2(b_last - bcum), TN)
    on = o * lax.rsqrt(jnp.mean(o * o, axis=-1, keepdims=True) + NORM_EPS) * nw
    return on * _silu(zg), st_new


def _hgrn_steps(c, hg_ref, vec_ref, o_ref, sfin_ref):
    tri = _tri_incl_bf16(c)
    qk = HG_HEADS * HG_DK

    def step(ci):
        r0 = ci * c
        for h in range(HG_HEADS):
            zq = hg_ref[r0:r0 + c, h * HG_DK:(h + 1) * HG_DK]
            zf = hg_ref[r0:r0 + c, qk + h * HG_DK:qk + (h + 1) * HG_DK]
            zi = hg_ref[r0:r0 + c, 2 * qk + h * HG_DV:2 * qk + (h + 1) * HG_DV]
            zg = hg_ref[r0:r0 + c, 2 * qk + HG_WIDTH + h * HG_DV:2 * qk + HG_WIDTH + (h + 1) * HG_DV]
            sl = slice(h * HG_DK, (h + 1) * HG_DK)
            out, st_new = _hgrn_chunk(zq, zf, zi, zg, sfin_ref[h], vec_ref[0:1, sl], vec_ref[1:2, sl],
                                      vec_ref[2:3, h * HG_DV:(h + 1) * HG_DV], tri)
            sfin_ref[h] = st_new
            o_ref[r0:r0 + c, h * HG_DV:(h + 1) * HG_DV] = out

    return step


RW_PASSES_A = 1
RW_PASSES_B = 1
RW_GROUP = 2


def _rwkv_chunk_matrices(c, chunks, tri, strict, incl, same_sub):
    pa = RW_PASSES_A
    prep = []
    for r, lw, km, v, kap, beta in chunks:
        cum = _mm_exact_lhs(tri, lw)
        c_last = cum[c - 1:c]
        e_neg = jnp.exp(-cum)
        e_hat = jnp.exp(c_last - cum)
        prep.append(dict(kbar=kap * jnp.exp(cum - lw), rbar=r * jnp.exp(cum), bneg=beta * e_neg,
                         kneg=km * e_neg, bhat=beta * e_hat, khat=km * e_hat, v=v, e_last=jnp.exp(c_last)))
    e_lasts = [pr['e_last'] for pr in prep]
    streams = [(ci, h) for ci in range(len(chunks)) for h in range(RW_HEADS)]
    hs = range(len(streams))

    def part(name):
        return [prep[ci][name][:, h * RW_HEAD:(h + 1) * RW_HEAD] for ci, h in streams]

    kbar, rbar, bneg, kneg, bhat, khat, vh = (part(n) for n in ('kbar', 'rbar', 'bneg', 'kneg', 'bhat', 'khat', 'v'))
    mask_kq = jnp.concatenate([strict, incl], axis=0)
    lhs = [jnp.concatenate([kbar[h], rbar[h]], axis=0) for h in hs]
    rhs = [jnp.concatenate([bneg[h], kneg[h]], axis=0) for h in hs]
    gmat = [_mm(lhs[h], rhs[h], NT, passes=pa) for h in hs]
    aab = [jnp.where(strict, gmat[h][:c, :c], 0.0) for h in hs]
    aqb = [jnp.where(incl, gmat[h][c:, :c], 0.0) for h in hs]
    akq = [jnp.where(mask_kq, gmat[h][:, c:], 0.0) for h in hs]
    av = [_mm(akq[h], vh[h], passes=pa) for h in hs]
    d1 = [jnp.where(same_sub, aab[h], 0.0) for h in hs]
    if c == SUB:
        z = [jnp.concatenate([kbar[h], av[h][:c]], axis=1) for h in hs]
    else:
        z = [jnp.concatenate([aab[h] - d1[h], kbar[h], av[h][:c]], axis=1) for h in hs]
    d2 = [_mm(d1[h], d1[h], passes=pa) for h in hs]
    z = [z[h] - _mm(d1[h], z[h], passes=pa) for h in hs]
    d4 = [_mm(d2[h], d2[h], passes=pa) for h in hs]
    z = [z[h] + _mm(d2[h], z[h], passes=pa) for h in hs]
    d8 = [_mm(d4[h], d4[h], passes=pa) for h in hs]
    z = [z[h] + _mm(d4[h], z[h], passes=pa) for h in hs]
    z = [z[h] + _mm(d8[h], z[h], passes=pa) for h in hs]
    if c == SUB:
        x = z
    else:
        f = [z[h][:, :c] for h in hs]
        x = [z[h][:, c:] for h in hs]
        fx = [_mm(f[h], x[h], passes=pa) for h in hs]
        x = [x[h] + _mm(f[h], fx[h], passes=pa) for h in hs]
        x = [x[h] - _mm(f[h], x[h], passes=pa) for h in hs]
    qb = [_mm(aqb[h], x[h], passes=pa) for h in hs]
    rt = [rbar[h] - qb[h][:, :RW_HEAD] for h in hs]
    yc = [av[h][c:] - qb[h][:, RW_HEAD:] for h in hs]
    ktb = [_mm(x[h][:, :RW_HEAD], bhat[h], TN, passes=pa) for h in hs]
    nc = [_mm(jnp.concatenate([vh[h], x[h][:, RW_HEAD:]], axis=0),
              jnp.concatenate([khat[h], -bhat[h]], axis=0), TN, passes=pa) for h in hs]
    return rt, yc, ktb, nc, e_lasts


def _rwkv_steps(c, rw_ref, vec_ref, o_ref, sfin_ref):
    tri = _tri_incl_bf16(c)
    row = lax.broadcasted_iota(jnp.int32, (c, c), 0)
    col = lax.broadcasted_iota(jnp.int32, (c, c), 1)
    strict = row > col
    incl = row >= col
    same_sub = (row // SUB) == (col // SUB)
    w = RW_WIDTH
    hs = range(RW_HEADS)
    sl = [slice(h * RW_HEAD, (h + 1) * RW_HEAD) for h in hs]
    state = [sfin_ref[h] for h in hs]
    n_chunks = rw_ref.shape[0] // c

    def step(g0):
        group = range(g0, min(g0 + RW_GROUP, n_chunks))
        chunks = [tuple(rw_ref[ci * c:(ci + 1) * c, j * w:(j + 1) * w] for j in range(6)) for ci in group]
        rt, yc, ktb, nc, e_lasts = _rwkv_chunk_matrices(c, chunks, tri, strict, incl, same_sub)
        for gi, ci in enumerate(group):
            r0 = ci * c
            r, _, km, v, _, _ = chunks[gi]
            gate = rw_ref[r0:r0 + c, 6 * w:7 * w]
            q = gi * RW_HEADS
            y = [_mm(rt[q + h], state[h], NT, passes=RW_PASSES_B) + yc[q + h] for h in hs]
            state[:] = [state[h] * e_lasts[gi][:, sl[h]] - _mm(state[h], ktb[q + h], passes=RW_PASSES_B)
                        + nc[q + h] for h in hs]
            bonus = r * km * vec_ref[2:3, :]
            for h in hs:
                mean = jnp.mean(y[h], axis=-1, keepdims=True)
                yd = y[h] - mean
                var = jnp.mean(yd * yd, axis=-1, keepdims=True)
                yn = yd * lax.rsqrt(var + RW_GN_EPS) * vec_ref[0:1, sl[h]] + vec_ref[1:2, sl[h]]
                yn = yn + jnp.sum(bonus[:, sl[h]], axis=-1, keepdims=True) * v[:, sl[h]]
                o_ref[r0:r0 + c, sl[h]] = yn * gate[:, sl[h]]

    def finish():
        for h in hs:
            sfin_ref[h] = state[h]

    return step, finish


def _hgrn_kernel(c, hg_ref, vec_ref, s0_ref, o_ref, sfin_ref):
    @pl.when(pl.program_id(1) == 0)
    def _():
        sfin_ref[...] = s0_ref[...]

    step = _hgrn_steps(c, hg_ref, vec_ref, o_ref, sfin_ref)
    for ci in range(hg_ref.shape[0] // c):
        step(ci)


def _rwkv_kernel(c, rw_ref, vec_ref, s0_ref, o_ref, sfin_ref):
    @pl.when(pl.program_id(1) == 0)
    def _():
        sfin_ref[...] = s0_ref[...]

    step, finish = _rwkv_steps(c, rw_ref, vec_ref, o_ref, sfin_ref)
    for g0 in range(0, rw_ref.shape[0] // c, RW_GROUP):
        step(g0)
    finish()


def _recurrence(kernel, name, x, vec, s0, width, ch, c):
    b, l, xw = x.shape
    state = pl.BlockSpec((None,) + s0.shape[1:], lambda i, j: (i, 0, 0, 0))
    return pl.pallas_call(
        functools.partial(kernel, c),
        grid=(b, l // ch),
        in_specs=[
            pl.BlockSpec((None, ch, xw), lambda i, j: (i, j, 0)),
            pl.BlockSpec(vec.shape, lambda i, j: (0, 0)),
            state,
        ],
        out_specs=[pl.BlockSpec((None, ch, width), lambda i, j: (i, j, 0)), state],
        out_shape=[jax.ShapeDtypeStruct((b, l, width), F32), jax.ShapeDtypeStruct(s0.shape, F32)],
        compiler_params=_params(("parallel", "arbitrary")),
        name=name,
    )(x, vec, s0)


def _postmix_kernel(ohg_ref, yrw_ref, x_ref, bvec_ref, nw_ref, wout_ref, wr_ref, br_ref,
                    xn_ref, h2_ref, route_ref):
    m = (jnp.dot(ohg_ref[...].astype(BF16), wout_ref[:HG_WIDTH, :], preferred_element_type=F32)
         + jnp.dot(yrw_ref[...].astype(BF16), wout_ref[HG_WIDTH:, :], preferred_element_type=F32))
    xn = x_ref[...] + bvec_ref[0:1, :] * m
    xn_ref[...] = xn
    h2 = (xn * lax.rsqrt(jnp.mean(xn * xn, axis=-1, keepdims=True) + NORM_EPS) * nw_ref[...]
          * (1.0 + bvec_ref[2:3, :]) + bvec_ref[1:2, :])
    h2_ref[...] = h2.astype(BF16)

    logits = _mm(h2, wr_ref[...]) + br_ref[...]
    lane = lax.broadcasted_iota(jnp.int32, logits.shape, 1)
    neg = jnp.float32(-jnp.inf)
    big = jnp.int32(ROUTE_W)
    is_g = lane < N_GROUPS
    gl = jnp.where(is_g, logits, neg)
    gmax = jnp.max(gl, axis=-1, keepdims=True)
    grp = jnp.min(jnp.where(gl == gmax, lane, big), axis=-1, keepdims=True)
    p_grp = 1.0 / jnp.sum(jnp.where(is_g, jnp.exp(gl - gmax), 0.0), axis=-1, keepdims=True)
    e_lo = N_GROUPS + grp * EXPERTS_PER_GROUP
    in_grp = (lane >= e_lo) & (lane < e_lo + EXPERTS_PER_GROUP)
    el = jnp.where(in_grp, logits, neg)
    m1 = jnp.max(el, axis=-1, keepdims=True)
    i1 = jnp.min(jnp.where(el == m1, lane, big), axis=-1, keepdims=True)
    el2 = jnp.where(lane == i1, neg, el)
    m2 = jnp.max(el2, axis=-1, keepdims=True)
    i2 = jnp.min(jnp.where(el2 == m2, lane, big), axis=-1, keepdims=True)
    e21 = jnp.exp(m2 - m1)
    p1 = 1.0 / (1.0 + e21)
    p2 = e21 / (1.0 + e21)
    out = jnp.where(lane == 0, (i1 - N_GROUPS).astype(F32),
                    jnp.where(lane == 1, (i2 - N_GROUPS).astype(F32),
                              jnp.where(lane == 2, p_grp * p1,
                                        jnp.where(lane == 3, p_grp * p2, 0.0))))
    route_ref[...] = out[:, :8]


def _postmix(ohg, yrw, x, bvec, nw, wout, wr, br, tm):
    b, l, d = x.shape
    const = lambda shape: pl.BlockSpec(shape, lambda i, j: tuple(0 for _ in shape))
    return pl.pallas_call(
        _postmix_kernel,
        grid=(b, l // tm),
        in_specs=[
            pl.BlockSpec((None, tm, HG_WIDTH), lambda i, j: (i, j, 0)),
            pl.BlockSpec((None, tm, RW_WIDTH), lambda i, j: (i, j, 0)),
            pl.BlockSpec((None, tm, d), lambda i, j: (i, j, 0)),
            pl.BlockSpec((None, 8, d), lambda i, j: (i, 0, 0)),
            const(nw.shape), const(wout.shape), const(wr.shape), const(br.shape),
        ],
        out_specs=[
            pl.BlockSpec((None, tm, d), lambda i, j: (i, j, 0)),
            pl.BlockSpec((None, tm, d), lambda i, j: (i, j, 0)),
            pl.BlockSpec((None, tm, 8), lambda i, j: (i, j, 0)),
        ],
        out_shape=[
            jax.ShapeDtypeStruct((b, l, d), F32),
            jax.ShapeDtypeStruct((b, l, d), BF16),
            jax.ShapeDtypeStruct((b, l, 8), F32),
        ],
        compiler_params=_params(("parallel", "parallel")),
        name="postmix",
    )(ohg, yrw, x, bvec, nw, wout, wr, br)


def _expert_kernel(be_ref, nu_ref, xs_ref, wg_ref, wu_ref, wd_ref, o_ref):
    i = pl.program_id(0)

    @pl.when(i < nu_ref[0])
    def _():
        xb = xs_ref[...].astype(BF16)
        hg = jnp.dot(xb, wg_ref[...].astype(BF16), preferred_element_type=F32)
        hu = jnp.dot(xb, wu_ref[...].astype(BF16), preferred_element_type=F32)
        hdn = (_silu(hg) * hu).astype(BF16)
        o_ref[...] = jnp.dot(hdn, wd_ref[...].astype(BF16), preferred_element_type=F32)

    @pl.when(i >= nu_ref[0])
    def _():
        o_ref[...] = jnp.zeros_like(o_ref)


def _experts(xs, block_e, n_used, wg, wu, wd, layer, blk):
    p, d = xs.shape
    nb = p // blk
    grid_spec = pltpu.PrefetchScalarGridSpec(
        num_scalar_prefetch=2,
        grid=(nb,),
        in_specs=[
            pl.BlockSpec((blk, d), lambda i, be, nu: (i, 0)),
            pl.BlockSpec((None, None, d, D_EXPERT), lambda i, be, nu: (layer, be[i], 0, 0)),
            pl.BlockSpec((None, None, d, D_EXPERT), lambda i, be, nu: (layer, be[i], 0, 0)),
            pl.BlockSpec((None, None, D_EXPERT, d), lambda i, be, nu: (layer, be[i], 0, 0)),
        ],
        out_specs=pl.BlockSpec((blk, d), lambda i, be, nu: (i, 0)),
    )
    return pl.pallas_call(
        _expert_kernel,
        grid_spec=grid_spec,
        out_shape=jax.ShapeDtypeStruct((p, d), F32),
        compiler_params=_params(("arbitrary",)),
        name="experts",
    )(block_e, n_used, xs, wg, wu, wd)


def _combine_kernel(final, xn_ref, y0_ref, y1_ref, route_ref, bvec_ref, fw_ref, o_ref):
    rt = route_ref[...]
    moe = rt[:, 2:3] * y0_ref[...] + rt[:, 3:4] * y1_ref[...]
    x = xn_ref[...] + bvec_ref[3:4, :] * moe
    if final:
        x = x * lax.rsqrt(jnp.mean(x * x, axis=-1, keepdims=True) + NORM_EPS) * fw_ref[...]
    o_ref[...] = x


def _combine(xn, y0, y1, route, bvec, fw, final, tm):
    b, l, d = xn.shape
    row = lambda w: pl.BlockSpec((None, tm, w), lambda i, j: (i, j, 0))
    return pl.pallas_call(
        functools.partial(_combine_kernel, final),
        grid=(b, l // tm),
        in_specs=[row(d), row(d), row(d), row(8),
                  pl.BlockSpec((None, 8, d), lambda i, j: (i, 0, 0)),
                  pl.BlockSpec((1, d), lambda i, j: (0, 0))],
        out_specs=row(d),
        out_shape=jax.ShapeDtypeStruct((b, l, d), F32),
        compiler_params=_params(("parallel", "parallel")),
        name="combine",
    )(xn, y0, y1, route, bvec, fw)


def _dispatch(route, blk):
    t = route.shape[0]
    n = t * TOP_K
    flat_e = route[:, :TOP_K].astype(jnp.int32).reshape(n)
    experts = jnp.arange(N_EXPERTS, dtype=jnp.int32)
    iota_n = jnp.arange(n, dtype=jnp.int32)
    sorted_e, order = lax.sort_key_val(flat_e, iota_n)
    in_e = sorted_e[:, None] == experts[None, :]
    counts = jnp.sum(in_e.astype(jnp.int32), axis=0)
    padded = ((counts + blk - 1) // blk) * blk
    start = jnp.cumsum(counts) - counts
    pend = jnp.cumsum(padded)
    pstart = pend - padded
    dest = iota_n + jnp.sum(jnp.where(in_e, (pstart - start)[None, :], 0), axis=1)
    _, slot = lax.sort_key_val(order, dest)
    nb = -(-(n + N_EXPERTS * (blk - 1)) // blk)
    block_e = jnp.minimum(jnp.sum((pend // blk)[None, :] <= jnp.arange(nb, dtype=jnp.int32)[:, None], axis=1),
                          N_EXPERTS - 1).astype(jnp.int32)
    n_used = (pend[-1] // blk).astype(jnp.int32).reshape(1)
    row = jnp.arange(nb * blk, dtype=jnp.int32).reshape(nb, blk)
    off = row - pstart[block_e][:, None]
    src = jnp.clip(start[block_e][:, None] + off, 0, n - 1)
    buf_tok = jnp.where(off < counts[block_e][:, None], order[src] // TOP_K, row % t).reshape(nb * blk)
    return buf_tok, slot.reshape(t, TOP_K), block_e, n_used


def _layer_weights(p, l):
    f = lambda name: p[name][l]
    zero_d = jnp.zeros((D_MODEL,), F32)
    has_vres = l > 0
    vecd = jnp.stack([f('norm_mix_w'), p['rw_mu_lora'][l, 0], p['rw_mu_lora'][l, 1], p['rw_mu_lora'][l, 2],
                      p['rw_mu_vres'][l - 1] if has_vres else zero_d, zero_d, zero_d, zero_d])
    zero_w = jnp.zeros((RW_WIDTH,), F32)
    vec5 = jnp.stack([f('rw_w0'), f('rw_a0'), p['rw_v0'][l - 1] if has_vres else zero_w,
                      f('rw_kk'), f('rw_ka'), zero_w, zero_w, zero_w])
    loras = [f('rw_w1'), f('rw_a1'), f('rw_g1')] + ([p['rw_v1'][l - 1]] if has_vres else [])
    loras += [f('rw_w2'), f('rw_a2'), f('rw_g2')] + ([p['rw_v2'][l - 1]] if has_vres else [])
    loras = [w.astype(BF16) for w in loras]
    rwvec = jnp.stack([f('rw_ln_w'), f('rw_ln_b'), f('rw_rk').reshape(RW_WIDTH),
                       zero_w, zero_w, zero_w, zero_w, zero_w])
    wr = jnp.concatenate([f('router_g_w'),
                          jnp.transpose(f('router_e_w'), (1, 0, 2)).reshape(D_MODEL, N_EXPERTS),
                          jnp.zeros((D_MODEL, ROUTE_W - N_GROUPS - N_EXPERTS), F32)], axis=1)
    br = jnp.concatenate([f('router_g_b'), f('router_e_b').reshape(N_EXPERTS),
                          jnp.zeros((ROUTE_W - N_GROUPS - N_EXPERTS,), F32)]).reshape(1, ROUTE_W)
    return dict(
        vecd=vecd, vec5=vec5, mup=f('rw_mu_proj').reshape(1, RW_PROJ), win=f('w_in').astype(BF16),
        loras=loras, rwvec=rwvec, hg_nw=f('hg_norm_w').reshape(1, HG_WIDTH),
        nfw=f('norm_ffn_w').reshape(1, D_MODEL), wout=f('w_out').astype(BF16), wr=wr, br=br)


def _trunk(x, mod, s_hg, s_rw, h_prev, lw_all, expert_w, lbvecs, final_w):
    b, l, d = x.shape
    depth = len(lw_all)
    tm = min(l, ROW_TILE)
    ch = min(l, SCAN_ROWS)
    c = min(l, CHUNK)
    blk = EXPERT_BLOCK if 2 * b * l * TOP_K >= N_EXPERTS * EXPERT_BLOCK else EXPERT_BLOCK_SMALL
    head = lax.broadcasted_iota(jnp.int32, (RW_WIDTH, RW_WIDTH), 0) // RW_HEAD
    seg = (head == head.T).astype(BF16)
    new_hg, new_rw, new_shift = [], [], []
    rwp_first = None
    zrow = jnp.zeros((b, d), F32)
    for li in range(depth):
        w = lw_all[li]
        sh1, sc1, g1, sh2, sc2, g2 = jnp.split(mod[li], 6, axis=-1)
        bvec1 = jnp.stack([sh1, sc1, h_prev[li], zrow, zrow, zrow, zrow, zrow], axis=1)
        bvec2 = jnp.stack([g1, sh2, sc2, g2, zrow, zrow, zrow, zrow], axis=1)
        hgraw, rwp, hlast = _premix(x, bvec1, w['vecd'], w['vec5'], w['mup'], w['win'], w['loras'],
                                    seg, rwp_first if li > 0 else None, tm)
        if li == 0:
            rwp_first = rwp
        hgvec = jnp.concatenate([lbvecs[li][:2], w['hg_nw'], lbvecs[li][3:]], axis=0)
        ohg, hg_t = _recurrence(_hgrn_kernel, "hgrn", hgraw, hgvec, jnp.swapaxes(s_hg[li], -1, -2),
                                HG_WIDTH, ch, c)
        yrw, rw_fin = _recurrence(_rwkv_kernel, "rwkv", rwp, w['rwvec'], s_rw[li], RW_WIDTH, ch, c)
        xn, h2, route = _postmix(ohg, yrw, x, bvec2, w['nfw'], w['wout'], w['wr'], w['br'], tm)
        t = b * l
        buf_tok, slot, block_e, n_used = _dispatch(route.reshape(t, 8), blk)
        xs = h2.reshape(t, d).at[buf_tok].get(mode='promise_in_bounds')
        yb = _experts(xs, block_e, n_used, *expert_w, li, blk)
        y0 = yb.at[slot[:, 0]].get(mode='promise_in_bounds').reshape(b, l, d)
        y1 = yb.at[slot[:, 1]].get(mode='promise_in_bounds').reshape(b, l, d)
        x = _combine(xn, y0, y1, route, bvec2, final_w, li == depth - 1, tm)
        new_hg.append(jnp.swapaxes(hg_t, -1, -2))
        new_rw.append(rw_fin)
        new_shift.append(hlast.reshape(b, d))
    return x, jnp.stack(new_hg), jnp.stack(new_rw), jnp.stack(new_shift)


def kernel(x_prompt, x_sample, state_hgrn, state_rwkv, state_shift, c_prompt, c_sample, w_ada, b_ada, norm_mix_w, norm_ffn_w, w_in, w_out, hg_lb_logits, hg_norm_w, rw_mu_proj, rw_mu_lora, rw_w0, rw_w1, rw_w2, rw_a0, rw_a1, rw_a2, rw_g1, rw_g2, rw_mu_vres, rw_v0, rw_v1, rw_v2, rw_kk, rw_ka, rw_rk, rw_ln_w, rw_ln_b, router_g_w, router_g_b, router_e_w, router_e_b, w_gate, w_up, w_down, final_norm_w):
    p = dict(norm_mix_w=norm_mix_w, norm_ffn_w=norm_ffn_w, w_in=w_in, w_out=w_out, hg_norm_w=hg_norm_w,
             rw_mu_proj=rw_mu_proj, rw_mu_lora=rw_mu_lora, rw_w0=rw_w0, rw_w1=rw_w1, rw_w2=rw_w2,
             rw_a0=rw_a0, rw_a1=rw_a1, rw_a2=rw_a2, rw_g1=rw_g1, rw_g2=rw_g2,
             rw_mu_vres=rw_mu_vres, rw_v0=rw_v0, rw_v1=rw_v1, rw_v2=rw_v2,
             rw_kk=rw_kk, rw_ka=rw_ka, rw_rk=rw_rk, rw_ln_w=rw_ln_w, rw_ln_b=rw_ln_b,
             router_g_w=router_g_w, router_g_b=router_g_b, router_e_w=router_e_w,
             router_e_b=router_e_b, w_gate=w_gate, w_up=w_up, w_down=w_down)
    depth = w_in.shape[0]
    bp = x_prompt.shape[0]
    bs = x_sample.shape[0]
    d = x_prompt.shape[-1]
    lw_all = [_layer_weights(p, l) for l in range(depth)]
    lb = jnp.cumsum(jax.nn.softmax(hg_lb_logits.astype(F32), axis=0), axis=0)
    lb = lb - lb[0]
    zpad = jnp.zeros((6, lb.shape[1]), F32)
    lbvecs = [jnp.concatenate([jnp.log(lb[l])[None], jnp.log1p(-lb[l])[None], zpad], axis=0)
              for l in range(depth)]
    rows = -(-(bp + bs) // 8) * 8
    c_all = jnp.concatenate([c_prompt, c_sample, jnp.zeros((rows - bp - bs, d), F32)], axis=0)
    mod = _ada_mod(c_all, w_ada, b_ada)
    fw = final_norm_w.reshape(1, d)
    zero_hg = jnp.zeros((depth, bp) + state_hgrn.shape[2:], F32)
    zero_rw = jnp.zeros((depth, bp) + state_rwkv.shape[2:], F32)
    zero_sh = jnp.zeros((depth, bp, d), F32)
    expert_w = (w_gate, w_up, w_down)
    y_p, hg_p, rw_p, sh_p = _trunk(x_prompt, mod[:, :bp], zero_hg, zero_rw, zero_sh, lw_all, expert_w,
                                   lbvecs, fw)
    y_s, hg_s, rw_s, sh_s = _trunk(x_sample, mod[:, bp:bp + bs], state_hgrn, state_rwkv, state_shift,
                                   lw_all, expert_w, lbvecs, fw)
    return (y_p, y_s, hg_p, rw_p, sh_p, hg_s, rw_s, sh_s)
```

```python
import functools

import jax
import jax.numpy as jnp
from jax import lax
from jax.experimental import pallas as pl
from jax.experimental.pallas import tpu as pltpu

F32 = jnp.float32
BF16 = jnp.bfloat16

D_MODEL = 1024
HG_HEADS = 4
HG_DK = 128
HG_DV = 128
HG_WIDTH = HG_HEADS * HG_DV
RW_HEADS = 8
RW_HEAD = 64
RW_WIDTH = RW_HEADS * RW_HEAD
HG_PROJ = 2 * HG_HEADS * HG_DK + 2 * HG_WIDTH
RW_PROJ = 3 * RW_WIDTH
IN_WIDTH = HG_PROJ + RW_PROJ
N_GROUPS = 4
EXPERTS_PER_GROUP = 8
N_EXPERTS = N_GROUPS * EXPERTS_PER_GROUP
TOP_K = 2
D_EXPERT = 512
LOG2E = 1.4426950408889634
NORM_EPS = 1e-6
RW_GN_EPS = 64e-5

ROW_TILE = 512
SCAN_ROWS = 256
HG_SCAN_ROWS = 512
EXPERT_BLOCK = 512
EXPERT_BLOCK_SMALL = 32
ADA_COLS = 1536
CHUNK = 64
SUB = 16
HG_SUB = 8
ROUTE_W = 128
RW_PACK = 7 * RW_WIDTH
VMEM_LIMIT = 56 * 1024 * 1024

NN = (((1,), (0,)), ((), ()))
NT = (((1,), (1,)), ((), ()))
TN = (((0,), (0,)), ((), ()))


def _split2(a):
    hi = a.astype(BF16)
    lo = (a - hi.astype(F32)).astype(BF16)
    return hi, lo


def _mm(a, b, dims=NN, passes=1):
    if passes == 1:
        return lax.dot_general(a.astype(BF16), b.astype(BF16), dims, preferred_element_type=F32)
    a1, a2 = _split2(a)
    b1, b2 = _split2(b)
    out = lax.dot_general(a1, b1, dims, preferred_element_type=F32)
    out = out + lax.dot_general(a1, b2, dims, preferred_element_type=F32)
    return out + lax.dot_general(a2, b1, dims, preferred_element_type=F32)


def _mm_exact_lhs(a_bf, b):
    b1 = b.astype(BF16)
    r = b - b1.astype(F32)
    b2 = r.astype(BF16)
    b3 = (r - b2.astype(F32)).astype(BF16)
    out = jnp.dot(a_bf, b1, preferred_element_type=F32)
    out = out + jnp.dot(a_bf, b2, preferred_element_type=F32)
    return out + jnp.dot(a_bf, b3, preferred_element_type=F32)


def _sigmoid(x):
    return 1.0 / (1.0 + jnp.exp(-x))


def _silu(x):
    return x * _sigmoid(x)


def _softplus(x):
    return jnp.maximum(x, 0.0) + jnp.log1p(jnp.exp(-jnp.abs(x)))


def _tri_incl_bf16(n):
    r = lax.broadcasted_iota(jnp.int32, (n, n), 0)
    c = lax.broadcasted_iota(jnp.int32, (n, n), 1)
    return (r >= c).astype(BF16)


def _params(sem):
    return pltpu.CompilerParams(dimension_semantics=sem, vmem_limit_bytes=VMEM_LIMIT)


def _ada_kernel(c_ref, w_ref, b_ref, o_ref):
    cs = _silu(c_ref[...])
    o_ref[...] = _mm(cs, w_ref[...]) + b_ref[...]


def _ada_mod(c_all, w_ada, b_ada):
    depth, d, n = w_ada.shape
    rows = c_all.shape[0]
    tn = ADA_COLS
    return pl.pallas_call(
        _ada_kernel,
        grid=(depth, n // tn),
        in_specs=[
            pl.BlockSpec((rows, d), lambda l, j: (0, 0)),
            pl.BlockSpec((None, d, tn), lambda l, j: (l, 0, j)),
            pl.BlockSpec((None, 1, tn), lambda l, j: (l, 0, j)),
        ],
        out_specs=pl.BlockSpec((None, rows, tn), lambda l, j: (l, 0, j)),
        out_shape=jax.ShapeDtypeStruct((depth, rows, n), F32),
        compiler_params=_params(("parallel", "parallel")),
        name="ada_mod",
    )(c_all, w_ada, b_ada.reshape(depth, 1, n))


def _premix_kernel(has_vres, *refs):
    if has_vres:
        (x_ref, bvec_ref, vecd_ref, vec5_ref, mup_ref, win_ref, w1_ref, a1_ref, g1_ref, v1_ref,
         w2_ref, a2_ref, g2_ref, v2_ref, seg_ref, vfirst_ref,
         hg_ref, rw_ref, hlast_ref, hcar, pcar) = refs
    else:
        (x_ref, bvec_ref, vecd_ref, vec5_ref, mup_ref, win_ref, w1_ref, a1_ref, g1_ref,
         w2_ref, a2_ref, g2_ref, seg_ref,
         hg_ref, rw_ref, hlast_ref, hcar, pcar) = refs
    t = pl.program_id(1)
    tm = x_ref.shape[0]
    w_rw = win_ref[:, HG_PROJ:]

    @pl.when(t == 0)
    def _():
        hp = jnp.broadcast_to(bvec_ref[2:3, :], (8, D_MODEL))
        hcar[...] = hp
        pcar[...] = _mm(hp, w_rw)

    x = x_ref[...]
    nw = vecd_ref[0:1, :]
    sh = bvec_ref[0:1, :]
    sc = bvec_ref[1:2, :]
    xn = x * lax.rsqrt(jnp.mean(x * x, axis=-1, keepdims=True) + NORM_EPS) * nw
    h = xn * (1.0 + sc) + sh

    hb = h.astype(BF16)
    hg_ref[...] = jnp.dot(hb, win_ref[:, :HG_PROJ], preferred_element_type=F32)
    p_cur = jnp.dot(hb, w_rw, preferred_element_type=F32)

    first = lax.broadcasted_iota(jnp.int32, (tm, 1), 0) == 0
    h_shift = jnp.where(first, hcar[0:1, :], pltpu.roll(h, 1, 0))
    p_shift = jnp.where(first, pcar[0:1, :], pltpu.roll(p_cur, 1, 0))
    hcar[0:1, :] = h[tm - 1:tm, :]
    pcar[0:1, :] = p_cur[tm - 1:tm, :]
    hlast_ref[...] = h[tm - 1:tm, :]

    pm = p_cur + (p_shift - p_cur) * mup_ref[...]
    r = pm[:, :RW_WIDTH]
    k = pm[:, RW_WIDTH:2 * RW_WIDTH]
    v = pm[:, 2 * RW_WIDTH:]

    dh = h_shift - h
    xw = h + dh * vecd_ref[1:2, :]
    xa = h + dh * vecd_ref[2:3, :]
    xg = h + dh * vecd_ref[3:4, :]
    w_pre = vec5_ref[0:1, :] + _mm(jnp.tanh(_mm(xw, w1_ref[...])), w2_ref[...])
    logw = -jnp.exp(-_softplus(-w_pre) - 0.5)
    a = _sigmoid(vec5_ref[1:2, :] + _mm(_mm(xa, a1_ref[...]), a2_ref[...]))
    g = _mm(_sigmoid(_mm(xg, g1_ref[...])), g2_ref[...])
    if has_vres:
        xv = h + dh * vecd_ref[4:5, :]
        nu = _sigmoid(vec5_ref[2:3, :] + _mm(_mm(xv, v1_ref[...]), v2_ref[...]))
        v = v + (vfirst_ref[...] - v) * nu

    kk = k * vec5_ref[3:4, :]
    k2hi, k2lo = _split2(kk * kk)
    ss = (jnp.dot(k2hi, seg_ref[...], preferred_element_type=F32)
          + jnp.dot(k2lo, seg_ref[...], preferred_element_type=F32))
    kappa = kk / jnp.maximum(jnp.sqrt(ss), 1e-12)
    k_mod = k * (1.0 + (a - 1.0) * vec5_ref[4:5, :])

    rw_ref[:, 0 * RW_WIDTH:1 * RW_WIDTH] = r
    rw_ref[:, 1 * RW_WIDTH:2 * RW_WIDTH] = logw
    rw_ref[:, 2 * RW_WIDTH:3 * RW_WIDTH] = k_mod
    rw_ref[:, 3 * RW_WIDTH:4 * RW_WIDTH] = v
    rw_ref[:, 4 * RW_WIDTH:5 * RW_WIDTH] = kappa
    rw_ref[:, 5 * RW_WIDTH:6 * RW_WIDTH] = kappa * a
    rw_ref[:, 6 * RW_WIDTH:7 * RW_WIDTH] = g


def _premix(x, bvec, vecd, vec5, mup, win, loras, seg, vfirst, tm):
    b, l, d = x.shape
    has_vres = vfirst is not None
    const = lambda shape: pl.BlockSpec(shape, lambda i, j: tuple(0 for _ in shape),
                                       pipeline_mode=pl.Buffered(1))
    in_specs = [
        pl.BlockSpec((None, tm, d), lambda i, j: (i, j, 0)),
        pl.BlockSpec((None, 8, d), lambda i, j: (i, 0, 0)),
        const(vecd.shape), const(vec5.shape), const(mup.shape), const(win.shape),
    ]
    in_specs += [const(w.shape) for w in loras]
    in_specs.append(const(seg.shape))
    args = [x, bvec, vecd, vec5, mup, win, *loras, seg]
    if has_vres:
        in_specs.append(pl.BlockSpec((None, tm, RW_WIDTH), lambda i, j: (i, j, 3)))
        args.append(vfirst)
    return pl.pallas_call(
        functools.partial(_premix_kernel, has_vres),
        grid=(b, l // tm),
        in_specs=in_specs,
        out_specs=[
            pl.BlockSpec((None, tm, HG_PROJ), lambda i, j: (i, j, 0)),
            pl.BlockSpec((None, tm, RW_PACK), lambda i, j: (i, j, 0)),
            pl.BlockSpec((None, 1, d), lambda i, j: (i, 0, 0)),
        ],
        out_shape=[
            jax.ShapeDtypeStruct((b, l, HG_PROJ), F32),
            jax.ShapeDtypeStruct((b, l, RW_PACK), F32),
            jax.ShapeDtypeStruct((b, 1, d), F32),
        ],
        scratch_shapes=[pltpu.VMEM((8, d), F32), pltpu.VMEM((8, RW_PROJ), F32)],
        compiler_params=_params(("parallel", "arbitrary")),
        name="premix",
    )(*args)


def _hgrn_chunk(zq, zf, zi, zg, st, loglb, log1mlb, nw, tri):
    c = zq.shape[0]
    q = _silu(zq)
    bterm = log1mlb + (jnp.minimum(zf, 0.0) - jnp.log1p(jnp.exp(-jnp.abs(zf))))
    logf = jnp.maximum(loglb, bterm) + jnp.log1p(jnp.exp(-jnp.abs(loglb - bterm)))
    k = 1.0 - jnp.exp(logf)
    v = zi
    bcum = _mm_exact_lhs(tri, logf) * LOG2E
    row = lax.broadcasted_iota(jnp.int32, (c, 1), 0)
    rr = lax.broadcasted_iota(jnp.int32, (c, c), 0)
    cc = lax.broadcasted_iota(jnp.int32, (c, c), 1)
    a = jnp.zeros((c, c), F32)
    m = c // 2
    while m >= HG_SUB:
        bmid = jnp.concatenate([jnp.broadcast_to(bcum[(2 * j + 1) * m - 1:(2 * j + 1) * m], (2 * m, HG_DK))
                                for j in range(c // (2 * m))], axis=0)
        upper = ((row // m) % 2) == 1
        qt = jnp.where(upper, q * jnp.exp2(bcum - bmid), 0.0)
        kt = jnp.where(upper, 0.0, k * jnp.exp2(bmid - bcum))
        a = a + jnp.where((rr // (2 * m)) == (cc // (2 * m)), _mm(qt, kt, NT), 0.0)
        m //= 2
    lane = lax.broadcasted_iota(jnp.int32, (HG_SUB, c), 1)
    rws = lax.broadcasted_iota(jnp.int32, (HG_SUB, c), 0)
    diag = []
    for i in range(c // HG_SUB):
        lo = i * HG_SUB
        bi = bcum[lo:lo + HG_SUB]
        qi = q[lo:lo + HG_SUB]
        ki = k[lo:lo + HG_SUB]
        d = jnp.zeros((HG_SUB, c), F32)
        for s in range(HG_SUB):
            dec = jnp.exp2(bi - bi[s:s + 1])
            col = jnp.sum(qi * ki[s:s + 1] * dec, axis=-1, keepdims=True)
            d = jnp.where((lane == lo + s) & (rws >= s), col, d)
        diag.append(d)
    a = a + jnp.concatenate(diag, axis=0)
    o = _mm(q * jnp.exp2(bcum), st, NT) + _mm(a, v)
    b_last = bcum[c - 1:c]
    st_new = st * jnp.exp2(b_last) + _mm(v, k * jnp.exp2(b_last - bcum), TN)
    on = o * lax.rsqrt(jnp.mean(o * o, axis=-1, keepdims=True) + NORM_EPS) * nw
    return on * _silu(zg), st_new


def _hgrn_steps(c, hg_ref, vec_ref, o_ref, sfin_ref):
    tri = _tri_incl_bf16(c)
    qk = HG_HEADS * HG_DK

    def step(ci):
        r0 = ci * c
        for h in range(HG_HEADS):
            zq = hg_ref[r0:r0 + c, h * HG_DK:(h + 1) * HG_DK]
            zf = hg_ref[r0:r0 + c, qk + h * HG_DK:qk + (h + 1) * HG_DK]
            zi = hg_ref[r0:r0 + c, 2 * qk + h * HG_DV:2 * qk + (h + 1) * HG_DV]
            zg = hg_ref[r0:r0 + c, 2 * qk + HG_WIDTH + h * HG_DV:2 * qk + HG_WIDTH + (h + 1) * HG_DV]
            sl = slice(h * HG_DK, (h + 1) * HG_DK)
            out, st_new = _hgrn_chunk(zq, zf, zi, zg, sfin_ref[h], vec_ref[0:1, sl], vec_ref[1:2, sl],
                                      vec_ref[2:3, h * HG_DV:(h + 1) * HG_DV], tri)
            sfin_ref[h] = st_new
            o_ref[r0:r0 + c, h * HG_DV:(h + 1) * HG_DV] = out

    return step


RW_PASSES_A = 1
RW_PASSES_B = 1
RW_GROUP = 2


def _rwkv_chunk_matrices(c, chunks, tri, strict, incl, same_sub):
    pa = RW_PASSES_A
    prep = []
    for r, lw, km, v, kap, beta in chunks:
        cum = _mm_exact_lhs(tri, lw)
        c_last = cum[c - 1:c]
        e_neg = jnp.exp(-cum)
        e_hat = jnp.exp(c_last - cum)
        prep.append(dict(kbar=kap * jnp.exp(cum - lw), rbar=r * jnp.exp(cum), bneg=beta * e_neg,
                         kneg=km * e_neg, bhat=beta * e_hat, khat=km * e_hat, v=v, e_last=jnp.exp(c_last)))
    e_lasts = [pr['e_last'] for pr in prep]
    streams = [(ci, h) for ci in range(len(chunks)) for h in range(RW_HEADS)]
    hs = range(len(streams))

    def part(name):
        return [prep[ci][name][:, h * RW_HEAD:(h + 1) * RW_HEAD] for ci, h in streams]

    kbar, rbar, bneg, kneg, bhat, khat, vh = (part(n) for n in ('kbar', 'rbar', 'bneg', 'kneg', 'bhat', 'khat', 'v'))
    mask_kq = jnp.concatenate([strict, incl], axis=0)
    lhs = [jnp.concatenate([kbar[h], rbar[h]], axis=0) for h in hs]
    rhs = [jnp.concatenate([bneg[h], kneg[h]], axis=0) for h in hs]
    gmat = [_mm(lhs[h], rhs[h], NT, passes=pa) for h in hs]
    aab = [jnp.where(strict, gmat[h][:c, :c], 0.0) for h in hs]
    aqb = [jnp.where(incl, gmat[h][c:, :c], 0.0) for h in hs]
    akq = [jnp.where(mask_kq, gmat[h][:, c:], 0.0) for h in hs]
    av = [_mm(akq[h], vh[h], passes=pa) for h in hs]
    d1 = [jnp.where(same_sub, aab[h], 0.0) for h in hs]
    if c == SUB:
        z = [jnp.concatenate([kbar[h], av[h][:c]], axis=1) for h in hs]
    else:
        z = [jnp.concatenate([aab[h] - d1[h], kbar[h], av[h][:c]], axis=1) for h in hs]
    d2 = [_mm(d1[h], d1[h], passes=pa) for h in hs]
    z = [z[h] - _mm(d1[h], z[h], passes=pa) for h in hs]
    d4 = [_mm(d2[h], d2[h], passes=pa) for h in hs]
    z = [z[h] + _mm(d2[h], z[h], passes=pa) for h in hs]
    d8 = [_mm(d4[h], d4[h], passes=pa) for h in hs]
    z = [z[h] + _mm(d4[h], z[h], passes=pa) for h in hs]
    z = [z[h] + _mm(d8[h], z[h], passes=pa) for h in hs]
    if c == SUB:
        x = z
    else:
        f = [z[h][:, :c] for h in hs]
        x = [z[h][:, c:] for h in hs]
        fx = [_mm(f[h], x[h], passes=pa) for h in hs]
        x = [x[h] + _mm(f[h], fx[h], passes=pa) for h in hs]
        x = [x[h] - _mm(f[h], x[h], passes=pa) for h in hs]
    qb = [_mm(aqb[h], x[h], passes=pa) for h in hs]
    rt = [rbar[h] - qb[h][:, :RW_HEAD] for h in hs]
    yc = [av[h][c:] - qb[h][:, RW_HEAD:] for h in hs]
    ktb = [_mm(x[h][:, :RW_HEAD], bhat[h], TN, passes=pa) for h in hs]
    nc = [_mm(jnp.concatenate([vh[h], x[h][:, RW_HEAD:]], axis=0),
              jnp.concatenate([khat[h], -bhat[h]], axis=0), TN, passes=pa) for h in hs]
    return rt, yc, ktb, nc, e_lasts


def _rwkv_steps(c, rw_ref, vec_ref, o_ref, sfin_ref):
    tri = _tri_incl_bf16(c)
    row = lax.broadcasted_iota(jnp.int32, (c, c), 0)
    col = lax.broadcasted_iota(jnp.int32, (c, c), 1)
    strict = row > col
    incl = row >= col
    same_sub = (row // SUB) == (col // SUB)
    w = RW_WIDTH
    hs = range(RW_HEADS)
    sl = [slice(h * RW_HEAD, (h + 1) * RW_HEAD) for h in hs]
    state = [sfin_ref[h] for h in hs]
    n_chunks = rw_ref.shape[0] // c

    def step(g0):
        group = range(g0, min(g0 + RW_GROUP, n_chunks))
        chunks = [tuple(rw_ref[ci * c:(ci + 1) * c, j * w:(j + 1) * w] for j in range(6)) for ci in group]
        rt, yc, ktb, nc, e_lasts = _rwkv_chunk_matrices(c, chunks, tri, strict, incl, same_sub)
        for gi, ci in enumerate(group):
            r0 = ci * c
            r, _, km, v, _, _ = chunks[gi]
            gate = rw_ref[r0:r0 + c, 6 * w:7 * w]
            q = gi * RW_HEADS
            y = [_mm(rt[q + h], state[h], NT, passes=RW_PASSES_B) + yc[q + h] for h in hs]
            state[:] = [state[h] * e_lasts[gi][:, sl[h]] - _mm(state[h], ktb[q + h], passes=RW_PASSES_B)
                        + nc[q + h] for h in hs]
            bonus = r * km * vec_ref[2:3, :]
            for h in hs:
                mean = jnp.mean(y[h], axis=-1, keepdims=True)
                yd = y[h] - mean
                var = jnp.mean(yd * yd, axis=-1, keepdims=True)
                yn = yd * lax.rsqrt(var + RW_GN_EPS) * vec_ref[0:1, sl[h]] + vec_ref[1:2, sl[h]]
                yn = yn + jnp.sum(bonus[:, sl[h]], axis=-1, keepdims=True) * v[:, sl[h]]
                o_ref[r0:r0 + c, sl[h]] = yn * gate[:, sl[h]]

    def finish():
        for h in hs:
            sfin_ref[h] = state[h]

    return step, finish


def _hgrn_kernel(c, hg_ref, vec_ref, s0_ref, o_ref, sfin_ref):
    @pl.when(pl.program_id(1) == 0)
    def _():
        sfin_ref[...] = s0_ref[...]

    step = _hgrn_steps(c, hg_ref, vec_ref, o_ref, sfin_ref)
    for ci in range(hg_ref.shape[0] // c):
        step(ci)


def _rwkv_kernel(c, rw_ref, vec_ref, s0_ref, o_ref, sfin_ref):
    @pl.when(pl.program_id(1) == 0)
    def _():
        sfin_ref[...] = s0_ref[...]

    step, finish = _rwkv_steps(c, rw_ref, vec_ref, o_ref, sfin_ref)
    for g0 in range(0, rw_ref.shape[0] // c, RW_GROUP):
        step(g0)
    finish()


def _recurrence(kernel, name, x, vec, s0, width, ch, c):
    b, l, xw = x.shape
    state = pl.BlockSpec((None,) + s0.shape[1:], lambda i, j: (i, 0, 0, 0))
    return pl.pallas_call(
        functools.partial(kernel, c),
        grid=(b, l // ch),
        in_specs=[
            pl.BlockSpec((None, ch, xw), lambda i, j: (i, j, 0)),
            pl.BlockSpec(vec.shape, lambda i, j: (0, 0)),
            state,
        ],
        out_specs=[pl.BlockSpec((None, ch, width), lambda i, j: (i, j, 0)), state],
        out_shape=[jax.ShapeDtypeStruct((b, l, width), F32), jax.ShapeDtypeStruct(s0.shape, F32)],
        compiler_params=_params(("parallel", "arbitrary")),
        name=name,
    )(x, vec, s0)


def _postmix_kernel(ohg_ref, yrw_ref, x_ref, bvec_ref, nw_ref, wout_ref, wr_ref, br_ref,
                    xn_ref, h2_ref, route_ref):
    m = (jnp.dot(ohg_ref[...].astype(BF16), wout_ref[:HG_WIDTH, :], preferred_element_type=F32)
         + jnp.dot(yrw_ref[...].astype(BF16), wout_ref[HG_WIDTH:, :], preferred_element_type=F32))
    xn = x_ref[...] + bvec_ref[0:1, :] * m
    xn_ref[...] = xn
    h2 = (xn * lax.rsqrt(jnp.mean(xn * xn, axis=-1, keepdims=True) + NORM_EPS) * nw_ref[...]
          * (1.0 + bvec_ref[2:3, :]) + bvec_ref[1:2, :])
    h2_ref[...] = h2.astype(BF16)

    logits = _mm(h2, wr_ref[...]) + br_ref[...]
    lane = lax.broadcasted_iota(jnp.int32, logits.shape, 1)
    neg = jnp.float32(-jnp.inf)
    big = jnp.int32(ROUTE_W)
    is_g = lane < N_GROUPS
    gl = jnp.where(is_g, logits, neg)
    gmax = jnp.max(gl, axis=-1, keepdims=True)
    grp = jnp.min(jnp.where(gl == gmax, lane, big), axis=-1, keepdims=True)
    p_grp = 1.0 / jnp.sum(jnp.where(is_g, jnp.exp(gl - gmax), 0.0), axis=-1, keepdims=True)
    e_lo = N_GROUPS + grp * EXPERTS_PER_GROUP
    in_grp = (lane >= e_lo) & (lane < e_lo + EXPERTS_PER_GROUP)
    el = jnp.where(in_grp, logits, neg)
    m1 = jnp.max(el, axis=-1, keepdims=True)
    i1 = jnp.min(jnp.where(el == m1, lane, big), axis=-1, keepdims=True)
    el2 = jnp.where(lane == i1, neg, el)
    m2 = jnp.max(el2, axis=-1, keepdims=True)
    i2 = jnp.min(jnp.where(el2 == m2, lane, big), axis=-1, keepdims=True)
    e21 = jnp.exp(m2 - m1)
    p1 = 1.0 / (1.0 + e21)
    p2 = e21 / (1.0 + e21)
    out = jnp.where(lane == 0, (i1 - N_GROUPS).astype(F32),
                    jnp.where(lane == 1, (i2 - N_GROUPS).astype(F32),
                              jnp.where(lane == 2, p_grp * p1,
                                        jnp.where(lane == 3, p_grp * p2, 0.0))))
    route_ref[...] = out[:, :8]


def _postmix(ohg, yrw, x, bvec, nw, wout, wr, br, tm):
    b, l, d = x.shape
    const = lambda shape: pl.BlockSpec(shape, lambda i, j: tuple(0 for _ in shape))
    return pl.pallas_call(
        _postmix_kernel,
        grid=(b, l // tm),
        in_specs=[
            pl.BlockSpec((None, tm, HG_WIDTH), lambda i, j: (i, j, 0)),
            pl.BlockSpec((None, tm, RW_WIDTH), lambda i, j: (i, j, 0)),
            pl.BlockSpec((None, tm, d), lambda i, j: (i, j, 0)),
            pl.BlockSpec((None, 8, d), lambda i, j: (i, 0, 0)),
            const(nw.shape), const(wout.shape), const(wr.shape), const(br.shape),
        ],
        out_specs=[
            pl.BlockSpec((None, tm, d), lambda i, j: (i, j, 0)),
            pl.BlockSpec((None, tm, d), lambda i, j: (i, j, 0)),
            pl.BlockSpec((None, tm, 8), lambda i, j: (i, j, 0)),
        ],
        out_shape=[
            jax.ShapeDtypeStruct((b, l, d), F32),
            jax.ShapeDtypeStruct((b, l, d), BF16),
            jax.ShapeDtypeStruct((b, l, 8), F32),
        ],
        compiler_params=_params(("parallel", "parallel")),
        name="postmix",
    )(ohg, yrw, x, bvec, nw, wout, wr, br)


def _expert_kernel(be_ref, nu_ref, xs_ref, wg_ref, wu_ref, wd_ref, o_ref):
    i = pl.program_id(0)

    @pl.when(i < nu_ref[0])
    def _():
        xb = xs_ref[...].astype(BF16)
        hg = jnp.dot(xb, wg_ref[...].astype(BF16), preferred_element_type=F32)
        hu = jnp.dot(xb, wu_ref[...].astype(BF16), preferred_element_type=F32)
        hdn = (_silu(hg) * hu).astype(BF16)
        o_ref[...] = jnp.dot(hdn, wd_ref[...].astype(BF16), preferred_element_type=F32)

    @pl.when(i >= nu_ref[0])
    def _():
        o_ref[...] = jnp.zeros_like(o_ref)


def _experts(xs, block_e, n_used, wg, wu, wd, layer, blk):
    p, d = xs.shape
    nb = p // blk
    grid_spec = pltpu.PrefetchScalarGridSpec(
        num_scalar_prefetch=2,
        grid=(nb,),
        in_specs=[
            pl.BlockSpec((blk, d), lambda i, be, nu: (i, 0)),
            pl.BlockSpec((None, None, d, D_EXPERT), lambda i, be, nu: (layer, be[i], 0, 0)),
            pl.BlockSpec((None, None, d, D_EXPERT), lambda i, be, nu: (layer, be[i], 0, 0)),
            pl.BlockSpec((None, None, D_EXPERT, d), lambda i, be, nu: (layer, be[i], 0, 0)),
        ],
        out_specs=pl.BlockSpec((blk, d), lambda i, be, nu: (i, 0)),
    )
    return pl.pallas_call(
        _expert_kernel,
        grid_spec=grid_spec,
        out_shape=jax.ShapeDtypeStruct((p, d), F32),
        compiler_params=_params(("arbitrary",)),
        name="experts",
    )(block_e, n_used, xs, wg, wu, wd)


def _combine_kernel(final, xn_ref, y0_ref, y1_ref, route_ref, bvec_ref, fw_ref, o_ref):
    rt = route_ref[...]
    moe = rt[:, 2:3] * y0_ref[...] + rt[:, 3:4] * y1_ref[...]
    x = xn_ref[...] + bvec_ref[3:4, :] * moe
    if final:
        x = x * lax.rsqrt(jnp.mean(x * x, axis=-1, keepdims=True) + NORM_EPS) * fw_ref[...]
    o_ref[...] = x


def _combine(xn, y0, y1, route, bvec, fw, final, tm):
    b, l, d = xn.shape
    row = lambda w: pl.BlockSpec((None, tm, w), lambda i, j: (i, j, 0))
    return pl.pallas_call(
        functools.partial(_combine_kernel, final),
        grid=(b, l // tm),
        in_specs=[row(d), row(d), row(d), row(8),
                  pl.BlockSpec((None, 8, d), lambda i, j: (i, 0, 0)),
                  pl.BlockSpec((1, d), lambda i, j: (0, 0))],
        out_specs=row(d),
        out_shape=jax.ShapeDtypeStruct((b, l, d), F32),
        compiler_params=_params(("parallel", "parallel")),
        name="combine",
    )(xn, y0, y1, route, bvec, fw)


def _dispatch(route, blk):
    t = route.shape[0]
    n = t * TOP_K
    flat_e = route[:, :TOP_K].astype(jnp.int32).reshape(n)
    experts = jnp.arange(N_EXPERTS, dtype=jnp.int32)
    iota_n = jnp.arange(n, dtype=jnp.int32)
    sorted_e, order = lax.sort_key_val(flat_e, iota_n)
    in_e = sorted_e[:, None] == experts[None, :]
    counts = jnp.sum(in_e.astype(jnp.int32), axis=0)
    padded = ((counts + blk - 1) // blk) * blk
    start = jnp.cumsum(counts) - counts
    pend = jnp.cumsum(padded)
    pstart = pend - padded
    dest = iota_n + jnp.sum(jnp.where(in_e, (pstart - start)[None, :], 0), axis=1)
    _, slot = lax.sort_key_val(order, dest)
    nb = -(-(n + N_EXPERTS * (blk - 1)) // blk)
    block_e = jnp.minimum(jnp.sum((pend // blk)[None, :] <= jnp.arange(nb, dtype=jnp.int32)[:, None], axis=1),
                          N_EXPERTS - 1).astype(jnp.int32)
    n_used = (pend[-1] // blk).astype(jnp.int32).reshape(1)
    row = jnp.arange(nb * blk, dtype=jnp.int32).reshape(nb, blk)
    off = row - pstart[block_e][:, None]
    src = jnp.clip(start[block_e][:, None] + off, 0, n - 1)
    buf_tok = jnp.where(off < counts[block_e][:, None], order[src] // TOP_K, row % t).reshape(nb * blk)
    return buf_tok, slot.reshape(t, TOP_K), block_e, n_used


def _layer_weights(p, l):
    f = lambda name: p[name][l]
    zero_d = jnp.zeros((D_MODEL,), F32)
    has_vres = l > 0
    vecd = jnp.stack([f('norm_mix_w'), p['rw_mu_lora'][l, 0], p['rw_mu_lora'][l, 1], p['rw_mu_lora'][l, 2],
                      p['rw_mu_vres'][l - 1] if has_vres else zero_d, zero_d, zero_d, zero_d])
    zero_w = jnp.zeros((RW_WIDTH,), F32)
    vec5 = jnp.stack([f('rw_w0'), f('rw_a0'), p['rw_v0'][l - 1] if has_vres else zero_w,
                      f('rw_kk'), f('rw_ka'), zero_w, zero_w, zero_w])
    loras = [f('rw_w1'), f('rw_a1'), f('rw_g1')] + ([p['rw_v1'][l - 1]] if has_vres else [])
    loras += [f('rw_w2'), f('rw_a2'), f('rw_g2')] + ([p['rw_v2'][l - 1]] if has_vres else [])
    loras = [w.astype(BF16) for w in loras]
    rwvec = jnp.stack([f('rw_ln_w'), f('rw_ln_b'), f('rw_rk').reshape(RW_WIDTH),
                       zero_w, zero_w, zero_w, zero_w, zero_w])
    wr = jnp.concatenate([f('router_g_w'),
                          jnp.transpose(f('router_e_w'), (1, 0, 2)).reshape(D_MODEL, N_EXPERTS),
                          jnp.zeros((D_MODEL, ROUTE_W - N_GROUPS - N_EXPERTS), F32)], axis=1)
    br = jnp.concatenate([f('router_g_b'), f('router_e_b').reshape(N_EXPERTS),
                          jnp.zeros((ROUTE_W - N_GROUPS - N_EXPERTS,), F32)]).reshape(1, ROUTE_W)
    return dict(
        vecd=vecd, vec5=vec5, mup=f('rw_mu_proj').reshape(1, RW_PROJ), win=f('w_in').astype(BF16),
        loras=loras, rwvec=rwvec, hg_nw=f('hg_norm_w').reshape(1, HG_WIDTH),
        nfw=f('norm_ffn_w').reshape(1, D_MODEL), wout=f('w_out').astype(BF16), wr=wr, br=br)


def _trunk(x, mod, s_hg, s_rw, h_prev, lw_all, expert_w, lbvecs, final_w):
    b, l, d = x.shape
    depth = len(lw_all)
    tm = min(l, ROW_TILE)
    ch = min(l, SCAN_ROWS)
    c = min(l, CHUNK)
    blk = EXPERT_BLOCK if 2 * b * l * TOP_K >= N_EXPERTS * EXPERT_BLOCK else EXPERT_BLOCK_SMALL
    head = lax.broadcasted_iota(jnp.int32, (RW_WIDTH, RW_WIDTH), 0) // RW_HEAD
    seg = (head == head.T).astype(BF16)
    new_hg, new_rw, new_shift = [], [], []
    rwp_first = None
    zrow = jnp.zeros((b, d), F32)
    for li in range(depth):
        w = lw_all[li]
        sh1, sc1, g1, sh2, sc2, g2 = jnp.split(mod[li], 6, axis=-1)
        bvec1 = jnp.stack([sh1, sc1, h_prev[li], zrow, zrow, zrow, zrow, zrow], axis=1)
        bvec2 = jnp.stack([g1, sh2, sc2, g2, zrow, zrow, zrow, zrow], axis=1)
        hgraw, rwp, hlast = _premix(x, bvec1, w['vecd'], w['vec5'], w['mup'], w['win'], w['loras'],
                                    seg, rwp_first if li > 0 else None, tm)
        if li == 0:
            rwp_first = rwp
        hgvec = jnp.concatenate([lbvecs[li][:2], w['hg_nw'], lbvecs[li][3:]], axis=0)
        ohg, hg_t = _recurrence(_hgrn_kernel, "hgrn", hgraw, hgvec, jnp.swapaxes(s_hg[li], -1, -2),
                                HG_WIDTH, min(l, HG_SCAN_ROWS), c)
        yrw, rw_fin = _recurrence(_rwkv_kernel, "rwkv", rwp, w['rwvec'], s_rw[li], RW_WIDTH, ch, c)
        xn, h2, route = _postmix(ohg, yrw, x, bvec2, w['nfw'], w['wout'], w['wr'], w['br'], tm)
        t = b * l
        buf_tok, slot, block_e, n_used = _dispatch(route.reshape(t, 8), blk)
        xs = h2.reshape(t, d).at[buf_tok].get(mode='promise_in_bounds')
        yb = _experts(xs, block_e, n_used, *expert_w, li, blk)
        y0 = yb.at[slot[:, 0]].get(mode='promise_in_bounds').reshape(b, l, d)
        y1 = yb.at[slot[:, 1]].get(mode='promise_in_bounds').reshape(b, l, d)
        x = _combine(xn, y0, y1, route, bvec2, final_w, li == depth - 1, tm)
        new_hg.append(jnp.swapaxes(hg_t, -1, -2))
        new_rw.append(rw_fin)
        new_shift.append(hlast.reshape(b, d))
    return x, jnp.stack(new_hg), jnp.stack(new_rw), jnp.stack(new_shift)


def kernel(x_prompt, x_sample, state_hgrn, state_rwkv, state_shift, c_prompt, c_sample, w_ada, b_ada, norm_mix_w, norm_ffn_w, w_in, w_out, hg_lb_logits, hg_norm_w, rw_mu_proj, rw_mu_lora, rw_w0, rw_w1, rw_w2, rw_a0, rw_a1, rw_a2, rw_g1, rw_g2, rw_mu_vres, rw_v0, rw_v1, rw_v2, rw_kk, rw_ka, rw_rk, rw_ln_w, rw_ln_b, router_g_w, router_g_b, router_e_w, router_e_b, w_gate, w_up, w_down, final_norm_w):
    p = dict(norm_mix_w=norm_mix_w, norm_ffn_w=norm_ffn_w, w_in=w_in, w_out=w_out, hg_norm_w=hg_norm_w,
             rw_mu_proj=rw_mu_proj, rw_mu_lora=rw_mu_lora, rw_w0=rw_w0, rw_w1=rw_w1, rw_w2=rw_w2,
             rw_a0=rw_a0, rw_a1=rw_a1, rw_a2=rw_a2, rw_g1=rw_g1, rw_g2=rw_g2,
             rw_mu_vres=rw_mu_vres, rw_v0=rw_v0, rw_v1=rw_v1, rw_v2=rw_v2,
             rw_kk=rw_kk, rw_ka=rw_ka, rw_rk=rw_rk, rw_ln_w=rw_ln_w, rw_ln_b=rw_ln_b,
             router_g_w=router_g_w, router_g_b=router_g_b, router_e_w=router_e_w,
             router_e_b=router_e_b, w_gate=w_gate, w_up=w_up, w_down=w_down)
    depth = w_in.shape[0]
    bp = x_prompt.shape[0]
    bs = x_sample.shape[0]
    d = x_prompt.shape[-1]
    lw_all = [_layer_weights(p, l) for l in range(depth)]
    lb = jnp.cumsum(jax.nn.softmax(hg_lb_logits.astype(F32), axis=0), axis=0)
    lb = lb - lb[0]
    zpad = jnp.zeros((6, lb.shape[1]), F32)
    lbvecs = [jnp.concatenate([jnp.log(lb[l])[None], jnp.log1p(-lb[l])[None], zpad], axis=0)
              for l in range(depth)]
    rows = -(-(bp + bs) // 8) * 8
    c_all = jnp.concatenate([c_prompt, c_sample, jnp.zeros((rows - bp - bs, d), F32)], axis=0)
    mod = _ada_mod(c_all, w_ada, b_ada)
    fw = final_norm_w.reshape(1, d)
    zero_hg = jnp.zeros((depth, bp) + state_hgrn.shape[2:], F32)
    zero_rw = jnp.zeros((depth, bp) + state_rwkv.shape[2:], F32)
    zero_sh = jnp.zeros((depth, bp, d), F32)
    expert_w = (w_gate, w_up, w_down)
    y_p, hg_p, rw_p, sh_p = _trunk(x_prompt, mod[:, :bp], zero_hg, zero_rw, zero_sh, lw_all, expert_w,
                                   lbvecs, fw)
    y_s, hg_s, rw_s, sh_s = _trunk(x_sample, mod[:, bp:bp + bs], state_hgrn, state_rwkv, state_shift,
                                   lw_all, expert_w, lbvecs, fw)
    return (y_p, y_s, hg_p, rw_p, sh_p, hg_s, rw_s, sh_s)
```
